```python
import jax, jax.numpy as jnp
from jax import lax
import numpy as np

D_MODEL = 2048
BATCH = 2
SEQ = 8192
DEPTH = 4

CHUNK = 64
D_FF = 5632
NORM_EPS = 1e-6

GDN_HEADS = 8
GDN_DK = 128
GDN_DV = 128
GDN_CONV = 4
GDN_QK = GDN_HEADS * GDN_DK
GDN_VW = GDN_HEADS * GDN_DV

RW_HEADS = 16
RW_HEAD = 64
RW_W = RW_HEADS * RW_HEAD
RW_DECAY_LORA = 96
RW_AAA_LORA = 96
RW_GATE_LORA = 256
RW_GN_EPS = 64e-5
RW_SLAB = 3 * RW_W + RW_DECAY_LORA + RW_AAA_LORA + RW_GATE_LORA

IN_SPLITS = (GDN_QK, GDN_QK, GDN_VW, GDN_VW, GDN_HEADS, GDN_HEADS, RW_SLAB, D_MODEL, D_MODEL)
N_IN = 2 * GDN_QK + 2 * GDN_VW + 2 * GDN_HEADS + RW_SLAB + 2 * D_MODEL

kernel_name = "hybrid_gdn_rwkv7_macaron"


def split_cols(t, sizes):
    out = []
    off = 0
    for s in sizes:
        out.append(t[..., off:off + s])
        off += s
    return out


def rmsnorm(x, gain):
    xf = x.astype(jnp.float32)
    y = xf * lax.rsqrt(jnp.mean(xf * xf, axis=-1, keepdims=True) + NORM_EPS)
    return (y * gain.astype(jnp.float32)).astype(x.dtype)


def l2norm(t, eps=1e-6):
    tf = t.astype(jnp.float32)
    return tf * lax.rsqrt(jnp.sum(tf * tf, axis=-1, keepdims=True) + eps)


def swiglu(h, w_gate, w_up, w_down):
    return (jax.nn.silu(h @ w_gate) * (h @ w_up)) @ w_down


def causal_dwconv(x, w):
    K = w.shape[0]
    xp = jnp.pad(x, ((0, 0), (K - 1, 0), (0, 0)))
    return lax.conv_general_dilated(xp, w[:, None, :].astype(x.dtype), window_strides=(1,), padding='VALID',
                                    dimension_numbers=('NWC', 'WIO', 'NWC'), feature_group_count=x.shape[-1])


def token_shift(x):
    return jnp.pad(x, ((0, 0), (1, 0), (0, 0)))[:, :-1]


def gdn_chunked(q, k, v, g, beta):
    B, S, H, DK = q.shape
    DV = v.shape[-1]
    N = S // CHUNK

    def blk(t):
        t = t.reshape((B, N, CHUNK, H) + t.shape[3:])
        return jnp.moveaxis(t, 3, 2)

    q, k, v, g, beta = blk(q), blk(k), blk(v), blk(g), blk(beta)
    gc = jnp.cumsum(g, axis=-1)
    incl = jnp.tril(jnp.ones((CHUNK, CHUNK), bool))
    strict = jnp.tril(jnp.ones((CHUNK, CHUNK), bool), -1)
    diff = gc[..., :, None] - gc[..., None, :]
    gamma = jnp.exp(jnp.where(incl, diff, -jnp.inf))
    kb = k * beta[..., None]
    A = jnp.where(strict, jnp.einsum('bnhid,bnhjd->bnhij', kb, k) * gamma, 0.0)
    eye = jnp.eye(CHUNK, dtype=jnp.float32)
    T = lax.linalg.triangular_solve(A + eye, jnp.broadcast_to(eye, A.shape), left_side=True,
                                    lower=True, unit_diagonal=True)
    U = jnp.einsum('bnhij,bnhjv->bnhiv', T, v * beta[..., None])
    W = jnp.einsum('bnhij,bnhjk->bnhik', T, kb * jnp.exp(gc)[..., None])
    Aqk = jnp.einsum('bnhid,bnhjd->bnhij', q, k) * gamma
    qg = q * jnp.exp(gc)[..., None]
    kg = k * jnp.exp(gc[..., -1:] - gc)[..., None]
    glast = jnp.exp(gc[..., -1])

    def step(state, inp):
        u_c, w_c, aqk_c, qg_c, kg_c, gl_c = inp
        v_new = u_c - jnp.einsum('bhck,bhkv->bhcv', w_c, state)
        o = jnp.einsum('bhck,bhkv->bhcv', qg_c, state) + jnp.einsum('bhij,bhjv->bhiv', aqk_c, v_new)
        state = state * gl_c[..., None, None] + jnp.einsum('bhck,bhcv->bhkv', kg_c, v_new)
        return state, o

    xs = tuple(jnp.moveaxis(t, 1, 0) for t in (U, W, Aqk, qg, kg, glast))
    _, o = lax.scan(step, jnp.zeros((B, H, DK, DV), jnp.float32), xs)
    return jnp.transpose(o, (1, 0, 3, 2, 4)).reshape(B, S, H, DV)


def rwkv7_scan(r, w, k, v, kk, a):
    B, S, H, N = r.shape

    def step(state, inp):
        r_t, w_t, k_t, v_t, kk_t, a_t = inp
        sk = jnp.einsum('bhvk,bhk->bhv', state, kk_t)
        state = (state * w_t[:, :, None, :] - sk[..., None] * (kk_t * a_t)[:, :, None, :]
                 + v_t[..., None] * k_t[:, :, None, :])
        return state, jnp.einsum('bhvk,bhk->bhv', state, r_t)

    xs = tuple(jnp.moveaxis(t, 1, 0) for t in (r, w, k, v, kk, a))
    _, y = lax.scan(step, jnp.zeros((B, H, N, N), jnp.float32), xs)
    return jnp.moveaxis(y, 0, 1)


def gdn_branch(q, k, v, z, a, b, conv_w, a_log, dt_bias, out_gain):
    B, S, _ = q.shape
    qkv = jax.nn.silu(causal_dwconv(jnp.concatenate([q, k, v], axis=-1), conv_w))
    q, k, v = split_cols(qkv, (GDN_QK, GDN_QK, GDN_VW))
    q = l2norm(q.reshape(B, S, GDN_HEADS, GDN_DK)) * (GDN_DK ** -0.5)
    k = l2norm(k.reshape(B, S, GDN_HEADS, GDN_DK))
    v = v.reshape(B, S, GDN_HEADS, GDN_DV).astype(jnp.float32)
    beta = jax.nn.sigmoid(b.astype(jnp.float32))
    g = -jnp.exp(a_log.astype(jnp.float32)) * jax.nn.softplus(a.astype(jnp.float32) + dt_bias.astype(jnp.float32))
    o = gdn_chunked(q, k, v, g, beta)
    o = rmsnorm(o, out_gain) * jax.nn.silu(z.reshape(B, S, GDN_HEADS, GDN_DV).astype(jnp.float32))
    return o.reshape(B, S, GDN_VW).astype(z.dtype)


def rwkv_branch(slab, mu, w0, w_up, a0, a_up, g_up, k_k, k_a, r_k, ln_w, ln_b):
    B, S, _ = slab.shape
    slab = slab + (token_shift(slab) - slab) * mu
    r, k, v, wl, al, gl = split_cols(slab, (RW_W, RW_W, RW_W, RW_DECAY_LORA, RW_AAA_LORA, RW_GATE_LORA))
    w = -jax.nn.softplus(-(w0 + jnp.tanh(wl) @ w_up)) - 0.5
    decay = jnp.exp(-jnp.exp(w.astype(jnp.float32)))
    a = jax.nn.sigmoid(a0 + al @ a_up)
    g = jax.nn.sigmoid(gl) @ g_up
    hs = (B, S, RW_HEADS, RW_HEAD)
    kk = l2norm((k * k_k).reshape(hs))
    k = k * (1.0 + (a - 1.0) * k_a)
    r4 = r.reshape(hs).astype(jnp.float32)
    k4 = k.reshape(hs).astype(jnp.float32)
    v4 = v.reshape(hs).astype(jnp.float32)
    y = rwkv7_scan(r4, decay.reshape(hs), k4, v4, kk, a.reshape(hs).astype(jnp.float32))
    mean = jnp.mean(y, axis=-1, keepdims=True)
    var = jnp.mean(jnp.square(y - mean), axis=-1, keepdims=True)
    y = ((y - mean) * lax.rsqrt(var + RW_GN_EPS)).reshape(B, S, RW_W)
    y = y * ln_w.astype(jnp.float32) + ln_b.astype(jnp.float32)
    bonus = jnp.sum(r4 * k4 * r_k.astype(jnp.float32), axis=-1, keepdims=True) * v4
    y = (y + bonus.reshape(B, S, RW_W)) * g.astype(jnp.float32)
    return y.astype(slab.dtype)


def setup_inputs(seed: int = 0) -> dict:
    key = jax.random.key(seed)
    ks = iter(jax.random.split(key, 48))
    L, D = DEPTH, D_MODEL

    def nrm(shape, scale):
        return jax.random.normal(next(ks), shape, jnp.float32) * scale

    def unif(shape, lo, hi):
        return jax.random.uniform(next(ks), shape, jnp.float32, minval=lo, maxval=hi)

    def gain(shape):
        return 1.0 + nrm(shape, 0.02)

    dt = jnp.exp(unif((L, GDN_HEADS), float(np.log(1e-3)), float(np.log(1e-1))))
    return {
        "x": nrm((BATCH, SEQ, D), 1.0),
        "ffn1_norm": gain((L, D)),
        "ffn1_w_gate": nrm((L, D, D_FF), D ** -0.5),
        "ffn1_w_up": nrm((L, D, D_FF), D ** -0.5),
        "ffn1_w_down": nrm((L, D_FF, D), D_FF ** -0.5),
        "mix_norm": gain((L, D)),
        "w_in": nrm((L, D, N_IN), D ** -0.5),
        "gdn_conv": nrm((L, GDN_CONV, 2 * GDN_QK + GDN_VW), GDN_CONV ** -0.5),
        "gdn_a_log": jnp.log(unif((L, GDN_HEADS), 1.0, 16.0)),
        "gdn_dt_bias": dt + jnp.log(-jnp.expm1(-dt)),
        "gdn_out_norm": gain((L, GDN_DV)),
        "rw_mu": unif((L, RW_SLAB), 0.0, 1.0),
        "rw_w0": unif((L, RW_W), -6.0, 1.0),
        "rw_w_up": nrm((L, RW_DECAY_LORA, RW_W), 0.5 * RW_DECAY_LORA ** -0.5),
        "rw_a0": nrm((L, RW_W), 0.1),
        "rw_a_up": nrm((L, RW_AAA_LORA, RW_W), 0.5 * RW_AAA_LORA ** -0.5),
        "rw_g_up": nrm((L, RW_GATE_LORA, RW_W), RW_GATE_LORA ** -0.5),
        "rw_k_k": 0.85 + nrm((L, RW_W), 0.02),
        "rw_k_a": 1.0 + nrm((L, RW_W), 0.02),
        "rw_r_k": nrm((L, RW_HEADS, RW_HEAD), 0.1),
        "rw_ln_w": gain((L, RW_W)),
        "rw_ln_b": nrm((L, RW_W), 0.02),
        "w_branch_a": nrm((L, GDN_VW, D), GDN_VW ** -0.5),
        "w_branch_b": nrm((L, RW_W, D), RW_W ** -0.5),
        "w_out": nrm((L, D, D), D ** -0.5),
        "ffn2_norm": gain((L, D)),
        "ffn2_w_gate": nrm((L, D, D_FF), D ** -0.5),
        "ffn2_w_up": nrm((L, D, D_FF), D ** -0.5),
        "ffn2_w_down": nrm((L, D_FF, D), D_FF ** -0.5),
        "final_norm": gain((D,)),
    }


def reference(x, ffn1_norm, ffn1_w_gate, ffn1_w_up, ffn1_w_down, mix_norm, w_in, gdn_conv, gdn_a_log,
              gdn_dt_bias, gdn_out_norm, rw_mu, rw_w0, rw_w_up, rw_a0, rw_a_up, rw_g_up, rw_k_k, rw_k_a,
              rw_r_k, rw_ln_w, rw_ln_b, w_branch_a, w_branch_b, w_out, ffn2_norm, ffn2_w_gate, ffn2_w_up,
              ffn2_w_down, final_norm):
    h = x
    for l in range(DEPTH):
        h = h + 0.5 * swiglu(rmsnorm(h, ffn1_norm[l]), ffn1_w_gate[l], ffn1_w_up[l], ffn1_w_down[l])
        u = rmsnorm(h, mix_norm[l])
        p = u @ w_in[l]
        q, k, v, z, a, b, slab, gate_a, gate_b = split_cols(p, IN_SPLITS)
        o_a = gdn_branch(q, k, v, z, a, b, gdn_conv[l], gdn_a_log[l], gdn_dt_bias[l], gdn_out_norm[l])
        o_b = rwkv_branch(slab, rw_mu[l], rw_w0[l], rw_w_up[l], rw_a0[l], rw_a_up[l], rw_g_up[l],
                          rw_k_k[l], rw_k_a[l], rw_r_k[l], rw_ln_w[l], rw_ln_b[l])
        y = jax.nn.sigmoid(gate_a) * (o_a @ w_branch_a[l]) + jax.nn.sigmoid(gate_b) * (o_b @ w_branch_b[l])
        h = h + y @ w_out[l]
        h = h + 0.5 * swiglu(rmsnorm(h, ffn2_norm[l]), ffn2_w_gate[l], ffn2_w_up[l], ffn2_w_down[l])
    return rmsnorm(h, final_norm)
```

```python
import functools

import jax
import jax.numpy as jnp
from jax import lax
from jax.experimental import pallas as pl
from jax.experimental.pallas import tpu as pltpu

F32 = jnp.float32
BF16 = jnp.bfloat16

V7X_LANES = 128
V7X_SUBLANES = 8
V7X_VMEM_BYTES = 64 * 1024 * 1024
VMEM_LIMIT_BYTES = V7X_VMEM_BYTES - 8 * 1024 * 1024

CHUNK = 64
NORM_EPS = 1e-6
L2_EPS = 1e-6
GDN_HEADS = 8
RW_HEADS = 16
RW_GN_EPS = 64e-5
SUB = 16


def _mm(a, b):
    return jnp.dot(a.astype(BF16), b.astype(BF16), preferred_element_type=F32)


def _mm_nt(a, b):
    return lax.dot_general(a.astype(BF16), b.astype(BF16), (((1,), (1,)), ((), ())),
                           preferred_element_type=F32)


def _split_bf16(x, terms):
    parts = []
    rem = x
    for _ in range(terms):
        p = rem.astype(BF16)
        parts.append(p)
        rem = rem - p.astype(F32)
    return parts


def _mm_wide_rhs(a_bf16, x, terms):
    acc = None
    for p in reversed(_split_bf16(x, terms)):
        t = jnp.dot(a_bf16, p, preferred_element_type=F32)
        acc = t if acc is None else acc + t
    return acc


def _mm_wide_lhs(x, b_bf16, terms):
    acc = None
    for p in reversed(_split_bf16(x, terms)):
        t = jnp.dot(p, b_bf16, preferred_element_type=F32)
        acc = t if acc is None else acc + t
    return acc


def _sigmoid(x):
    return jax.nn.sigmoid(x)


def _softplus(x):
    return jnp.maximum(x, 0.0) + jnp.log1p(jnp.exp(-jnp.abs(x)))


def _rmsnorm(x, gain, eps):
    ms = jnp.mean(x * x, axis=-1, keepdims=True)
    return x * lax.rsqrt(ms + eps) * gain


def _col(x, lane):
    idx = lax.broadcasted_iota(jnp.int32, x.shape, 1)
    return jnp.sum(jnp.where(idx == lane, x, 0.0), axis=-1, keepdims=True)


def _tri_masks(n):
    i = lax.broadcasted_iota(jnp.int32, (n, n), 0)
    j = lax.broadcasted_iota(jnp.int32, (n, n), 1)
    same = (i // CHUNK) == (j // CHUNK)
    incl = same & (j <= i)
    strict = same & (j < i)
    eye = jnp.where(i == j, 1.0, 0.0).astype(F32)
    lvl0 = (i // SUB) == (j // SUB)
    levels = []
    size = SUB
    while size < CHUNK:
        levels.append(((i // (2 * size)) == (j // (2 * size))) & ((i // size) != (j // size)))
        size *= 2
    return incl, strict, eye, lvl0, levels


def _unit_lower_inverse(a, eye, lvl0, levels):
    d = jnp.where(lvl0, a, 0.0)
    t = eye - d
    pw = d
    span = 2
    while span < SUB:
        pw = _mm(pw, pw)
        t = t + _mm(t, pw)
        span *= 2
    for m in levels:
        e = jnp.where(m, a, 0.0)
        t = t - _mm(_mm(t, e), t)
    return t


def _cparams(sem):
    return pltpu.CompilerParams(dimension_semantics=sem, vmem_limit_bytes=VMEM_LIMIT_BYTES)


def _ffn_kernel(x_ref, gain_ref, wg_ref, wu_ref, wd_ref, fgain_ref, o_ref, xn_ref, *, nj, final):
    j = pl.program_id(1)

    @pl.when(j == 0)
    def _():
        xn_ref[...] = _rmsnorm(x_ref[...], gain_ref[...], NORM_EPS).astype(BF16)
        o_ref[...] = jnp.zeros_like(o_ref)

    xn = xn_ref[...]
    g = jnp.dot(xn, wg_ref[...], preferred_element_type=F32)
    u = jnp.dot(xn, wu_ref[...], preferred_element_type=F32)
    act = (g * _sigmoid(g) * u).astype(BF16)
    o_ref[...] += jnp.dot(act, wd_ref[...], preferred_element_type=F32)

    @pl.when(j == nj - 1)
    def _():
        hn = x_ref[...] + 0.5 * o_ref[...]
        if final:
            hn = _rmsnorm(hn, fgain_ref[...], NORM_EPS)
        o_ref[...] = hn


def _ffn(h, gain, wg, wu, wd, fgain, *, final, tm, tf):
    m, d = h.shape
    f = wg.shape[1]
    nj = f // tf
    return pl.pallas_call(
        functools.partial(_ffn_kernel, nj=nj, final=final),
        out_shape=jax.ShapeDtypeStruct((m, d), F32),
        grid=(m // tm, nj),
        in_specs=[
            pl.BlockSpec((tm, d), lambda i, j: (i, 0)),
            pl.BlockSpec((1, d), lambda i, j: (0, 0)),
            pl.BlockSpec((d, tf), lambda i, j: (0, j)),
            pl.BlockSpec((d, tf), lambda i, j: (0, j)),
            pl.BlockSpec((tf, d), lambda i, j: (j, 0)),
            pl.BlockSpec((1, d), lambda i, j: (0, 0)),
        ],
        out_specs=pl.BlockSpec((tm, d), lambda i, j: (i, 0)),
        scratch_shapes=[pltpu.VMEM((tm, d), BF16)],
        compiler_params=_cparams(("parallel", "arbitrary")),
        name="ffn",
    )(h, gain, wg, wu, wd, fgain)


def _inproj_kernel(x_ref, gain_ref, w_ref, o_ref, xn_ref):
    @pl.when(pl.program_id(1) == 0)
    def _():
        xn_ref[...] = _rmsnorm(x_ref[...], gain_ref[...], NORM_EPS).astype(BF16)

    o_ref[...] = jnp.dot(xn_ref[...], w_ref[...], preferred_element_type=F32)


def _inproj(h, gain, w, *, tm, tn):
    m, d = h.shape
    n = w.shape[1]
    return pl.pallas_call(
        _inproj_kernel,
        out_shape=jax.ShapeDtypeStruct((m, n), F32),
        grid=(m // tm, n // tn),
        in_specs=[
            pl.BlockSpec((tm, d), lambda i, j: (i, 0)),
            pl.BlockSpec((1, d), lambda i, j: (0, 0)),
            pl.BlockSpec((d, tn), lambda i, j: (0, j)),
        ],
        out_specs=pl.BlockSpec((tm, tn), lambda i, j: (i, j)),
        scratch_shapes=[pltpu.VMEM((tm, d), BF16)],
        compiler_params=_cparams(("parallel", "arbitrary")),
        name="inproj",
    )(h, gain, w)


def _mixout_kernel(x_ref, gain_ref, oa_ref, ob_ref, wga_ref, wgb_ref, wba_ref, wbb_ref, wo_ref,
                   o_ref, xn_ref, *, nj):
    j = pl.program_id(1)

    @pl.when(j == 0)
    def _():
        xn_ref[...] = _rmsnorm(x_ref[...], gain_ref[...], NORM_EPS).astype(BF16)
        o_ref[...] = jnp.zeros_like(o_ref)

    xn = xn_ref[...]
    ga = jnp.dot(xn, wga_ref[...], preferred_element_type=F32)
    gb = jnp.dot(xn, wgb_ref[...], preferred_element_type=F32)
    ya = jnp.dot(oa_ref[...], wba_ref[...], preferred_element_type=F32)
    yb = jnp.dot(ob_ref[...], wbb_ref[...], preferred_element_type=F32)
    y = (_sigmoid(ga) * ya + _sigmoid(gb) * yb).astype(BF16)
    o_ref[...] += jnp.dot(y, wo_ref[...], preferred_element_type=F32)

    @pl.when(j == nj - 1)
    def _():
        o_ref[...] = x_ref[...] + o_ref[...]


def _mixout(h, gain, oa, ob, wga, wgb, wba, wbb, wo, *, tm, tj):
    m, d = h.shape
    na = oa.shape[1]
    nb = ob.shape[1]
    dy = wga.shape[1]
    nj = dy // tj
    return pl.pallas_call(
        functools.partial(_mixout_kernel, nj=nj),
        out_shape=jax.ShapeDtypeStruct((m, d), F32),
        grid=(m // tm, nj),
        in_specs=[
            pl.BlockSpec((tm, d), lambda i, j: (i, 0)),
            pl.BlockSpec((1, d), lambda i, j: (0, 0)),
            pl.BlockSpec((tm, na), lambda i, j: (i, 0)),
            pl.BlockSpec((tm, nb), lambda i, j: (i, 0)),
            pl.BlockSpec((d, tj), lambda i, j: (0, j)),
            pl.BlockSpec((d, tj), lambda i, j: (0, j)),
            pl.BlockSpec((na, tj), lambda i, j: (0, j)),
            pl.BlockSpec((nb, tj), lambda i, j: (0, j)),
            pl.BlockSpec((tj, d), lambda i, j: (j, 0)),
        ],
        out_specs=pl.BlockSpec((tm, d), lambda i, j: (i, 0)),
        scratch_shapes=[pltpu.VMEM((tm, d), BF16)],
        compiler_params=_cparams(("parallel", "arbitrary")),
        name="mixout",
    )(h, gain, oa, ob, wga, wgb, wba, wbb, wo)


def _gdn_kernel(qkvz_ref, ab_ref, convw_ref, alog_ref, dtb_ref, ogain_ref, o_ref, xpad_ref, state_ref,
                *, heads, dk, dv, taps, a_lane, b_lane):
    C = CHUNK
    PAD = V7X_SUBLANES
    nq = heads * dk
    nqkv = 2 * nq + heads * dv

    @pl.when(pl.program_id(1) == 0)
    def _():
        xpad_ref[0:PAD, :] = jnp.zeros((PAD, nqkv), F32)
        state_ref[...] = jnp.zeros_like(state_ref)

    x = qkvz_ref[:, 0:nqkv]
    xpad_ref[PAD:PAD + C, :] = x
    acc = None
    for j in range(taps):
        term = xpad_ref[pl.ds(PAD - (taps - 1) + j, C), :] * convw_ref[j:j + 1, :]
        acc = term if acc is None else acc + term
    xpad_ref[0:PAD, :] = x[C - PAD:C, :]
    y = acc * _sigmoid(acc)

    ab = ab_ref[...]
    g = -jnp.exp(alog_ref[...]) * _softplus(ab + dtb_ref[...])
    beta = _sigmoid(ab)
    incl, strict, eye, lvl0, levels = _tri_masks(C)
    tril = jnp.where(incl, 1.0, 0.0).astype(BF16)
    gc = _mm_wide_rhs(tril, g, 3)
    glast = gc[C - 1:C, :]
    e_pos = jnp.exp(gc)
    e_rem = jnp.exp(glast - gc)
    e_last = jnp.exp(glast)
    gc_t = gc.T

    for h in range(heads):
        qh = y[:, h * dk:(h + 1) * dk]
        kh = y[:, nq + h * dk:nq + (h + 1) * dk]
        vh = y[:, 2 * nq + h * dv:2 * nq + (h + 1) * dv]
        zh = qkvz_ref[:, nqkv + h * dv:nqkv + (h + 1) * dv]
        qh = qh * lax.rsqrt(jnp.sum(qh * qh, axis=-1, keepdims=True) + L2_EPS) * (dk ** -0.5)
        kh = kh * lax.rsqrt(jnp.sum(kh * kh, axis=-1, keepdims=True) + L2_EPS)
        la = a_lane + h
        gcol = _col(gc, la)
        ecol = _col(e_pos, la)
        rcol = _col(e_rem, la)
        elast = _col(e_last, la)
        bcol = _col(beta, b_lane + h)
        grow = gc_t[la:la + 1, :]
        gamma = jnp.where(incl, jnp.exp(jnp.where(incl, gcol - grow, 0.0)), 0.0)
        kb = kh * bcol
        sc = _mm_nt(jnp.concatenate([kb, qh], axis=0), kh)
        a = jnp.where(strict, sc[0:C] * gamma, 0.0)
        aqk = sc[C:2 * C] * gamma
        t = _unit_lower_inverse(a, eye, lvl0, levels)
        uw = _mm(t, jnp.concatenate([vh * bcol, kb * ecol], axis=1))
        u = uw[:, 0:dv]
        w = uw[:, dv:dv + dk]
        s = state_ref[h]
        ws = _mm(jnp.concatenate([w, qh * ecol], axis=0), s)
        vnew = u - ws[0:C]
        o = ws[C:2 * C] + _mm(aqk, vnew)
        state_ref[h] = s * elast + _mm((kh * rcol).T, vnew)
        on = _rmsnorm(o, ogain_ref[...], NORM_EPS)
        o_ref[:, h * dv:(h + 1) * dv] = (on * (zh * _sigmoid(zh))).astype(o_ref.dtype)


def _gdn(p, convw, alog_row, dtb_row, ogain, *, batch, seq, heads, dk, dv, ab_block, a_lane, b_lane):
    nc = seq // CHUNK
    nqkvz = 2 * heads * dk + 2 * heads * dv
    nqkv = nqkvz - heads * dv
    taps = convw.shape[0]
    return pl.pallas_call(
        functools.partial(_gdn_kernel, heads=heads, dk=dk, dv=dv, taps=taps, a_lane=a_lane, b_lane=b_lane),
        out_shape=jax.ShapeDtypeStruct((batch * seq, heads * dv), BF16),
        grid=(batch, nc),
        in_specs=[
            pl.BlockSpec((CHUNK, nqkvz), lambda b, c: (b * nc + c, 0)),
            pl.BlockSpec((CHUNK, V7X_LANES), lambda b, c: (b * nc + c, ab_block)),
            pl.BlockSpec((taps, nqkv), lambda b, c: (0, 0)),
            pl.BlockSpec((1, V7X_LANES), lambda b, c: (0, 0)),
            pl.BlockSpec((1, V7X_LANES), lambda b, c: (0, 0)),
            pl.BlockSpec((1, dv), lambda b, c: (0, 0)),
        ],
        out_specs=pl.BlockSpec((CHUNK, heads * dv), lambda b, c: (b * nc + c, 0)),
        scratch_shapes=[
            pltpu.VMEM((CHUNK + V7X_SUBLANES, nqkv), F32),
            pltpu.VMEM((heads, dk, dv), F32),
        ],
        compiler_params=_cparams(("parallel", "arbitrary")),
        name="gdn",
    )(p, p, convw, alog_row, dtb_row, ogain)


def _rwkv_kernel(r_ref, k_ref, v_ref, l_ref, mur_ref, muk_ref, muv_ref, mul_ref, w0_ref, wup_ref, a0_ref,
                 aup_ref, gup_ref, kk_ref, ka_ref, rk_ref, lnw_ref, lnb_ref, o_ref,
                 pr_ref, pk_ref, pv_ref, pl_ref, state_ref, *, heads, hd):
    C = CHUNK
    PAD = V7X_SUBLANES
    LW = V7X_LANES
    npairs = heads * hd // LW
    first = pl.program_id(1) == 0

    @pl.when(first)
    def _():
        for ref in (pr_ref, pk_ref, pv_ref, pl_ref):
            ref[...] = jnp.zeros_like(ref)
        state_ref[...] = jnp.zeros_like(state_ref)

    def shift_mix(x_ref, prev_ref, mu_ref):
        x = x_ref[...]
        row = lax.broadcasted_iota(jnp.int32, x.shape, 0)
        xprev = jnp.where(row == 0, prev_ref[PAD - 1:PAD, :], pltpu.roll(x, 1, 0))
        prev_ref[...] = x[C - PAD:C, :]
        return x + (xprev - x) * mu_ref[...]

    r = shift_mix(r_ref, pr_ref, mur_ref)
    k = shift_mix(k_ref, pk_ref, muk_ref)
    v = shift_mix(v_ref, pv_ref, muv_ref)
    lo = shift_mix(l_ref, pl_ref, mul_ref)

    wpre = w0_ref[...] + _mm(jnp.tanh(lo), wup_ref[...])
    lw = -jnp.exp(-_softplus(-wpre) - 0.5)
    a = _sigmoid(a0_ref[...] + _mm(lo, aup_ref[...]))
    gate = _mm(_sigmoid(lo), gup_ref[...])

    incl, strict, eye, lvl0, levels = _tri_masks(2 * C)
    li = lax.broadcasted_iota(jnp.int32, (LW, LW), 0)
    lj = lax.broadcasted_iota(jnp.int32, (LW, LW), 1)
    headsum = jnp.where((li // hd) == (lj // hd), 1.0, 0.0).astype(BF16)
    ci = lax.broadcasted_iota(jnp.int32, (C, C), 0)
    cj = lax.broadcasted_iota(jnp.int32, (C, C), 1)
    tril = jnp.where(cj <= ci, 1.0, 0.0).astype(BF16)
    lane = lax.broadcasted_iota(jnp.int32, (C, LW), 1)
    m0 = lane < hd

    kkraw = k * kk_ref[...]
    k2 = k * (1.0 + (a - 1.0) * ka_ref[...])
    cum = _mm_wide_rhs(tril, lw, 3)
    clast = cum[C - 1:C, :]
    e_pos = jnp.exp(cum)
    e_neg = jnp.exp(-cum)
    e_exc = jnp.exp(cum - lw)
    e_rem = jnp.exp(clast - cum)
    wc = jnp.exp(clast)
    rkr = r * k2 * rk_ref[...]

    def stack(x):
        return jnp.concatenate([jnp.where(m0, x, 0.0), jnp.where(m0, 0.0, x)], axis=0)

    for p in range(npairs):
        sl = slice(p * LW, (p + 1) * LW)
        kkr = kkraw[:, sl]
        sums = _mm_wide_lhs(jnp.concatenate([kkr * kkr, rkr[:, sl]], axis=0), headsum, 2)
        kk = kkr * lax.rsqrt(sums[0:C] + L2_EPS)
        bonus = sums[C:2 * C] * v[:, sl]
        a_s = stack(-kk * e_exc[:, sl])
        r_s = stack(r[:, sl] * e_pos[:, sl])
        kka = kk * a[:, sl]
        b_s = stack(kka * e_neg[:, sl])
        k_s = stack(k2[:, sl] * e_neg[:, sl])
        bb_s = stack(kka * e_rem[:, sl])
        kb_s = stack(k2[:, sl] * e_rem[:, sl])
        v_s = stack(v[:, sl])

        sc = _mm_nt(jnp.concatenate([a_s, r_s], axis=0), jnp.concatenate([b_s, k_s], axis=0))
        n_ab = jnp.where(strict, sc[0:2 * C, 0:2 * C], 0.0)
        l_ak = jnp.where(strict, sc[0:2 * C, 2 * C:4 * C], 0.0)
        m_rb = jnp.where(incl, sc[2 * C:4 * C, 0:2 * C], 0.0)
        m_rk = jnp.where(incl, sc[2 * C:4 * C, 2 * C:4 * C], 0.0)
        t = _unit_lower_inverse(-n_ab, eye, lvl0, levels)
        tt = _mm(t, jnp.concatenate([a_s, _mm(l_ak, v_s)], axis=1))
        q = state_ref[p]
        pq = _mm_nt(jnp.concatenate([tt[:, 0:LW], r_s], axis=0), q)
        u = pq[0:2 * C] + tt[:, LW:2 * LW]
        uv = jnp.concatenate([u, v_s], axis=0)
        y_s = pq[2 * C:4 * C] + _mm(jnp.concatenate([m_rb, m_rk], axis=1), uv)
        state_ref[p] = q * wc[:, sl] + _mm(uv.T, jnp.concatenate([bb_s, kb_s], axis=0))
        yp = y_s[0:C] + y_s[C:2 * C]

        mean = _mm_wide_lhs(yp, headsum, 2) * (1.0 / hd)
        dlt = yp - mean
        var = _mm_wide_lhs(dlt * dlt, headsum, 2) * (1.0 / hd)
        yn = dlt * lax.rsqrt(var + RW_GN_EPS) * lnw_ref[:, sl] + lnb_ref[:, sl]
        o_ref[:, sl] = ((yn + bonus) * gate[:, sl]).astype(o_ref.dtype)


def _rwkv(p, mu_r, mu_k, mu_v, mu_l, w0, wup, a0, aup, gup, k_k, k_a, r_k, ln_w, ln_b,
          *, batch, seq, heads, hd, r_block, lora_block, lora_width):
    nc = seq // CHUNK
    width = heads * hd
    row = lambda b, c: b * nc + c
    const = lambda b, c: (0, 0)
    vec = pl.BlockSpec((1, width), const)
    return pl.pallas_call(
        functools.partial(_rwkv_kernel, heads=heads, hd=hd),
        out_shape=jax.ShapeDtypeStruct((batch * seq, width), BF16),
        grid=(batch, nc),
        in_specs=[
            pl.BlockSpec((CHUNK, width), lambda b, c: (row(b, c), r_block)),
            pl.BlockSpec((CHUNK, width), lambda b, c: (row(b, c), r_block + 1)),
            pl.BlockSpec((CHUNK, width), lambda b, c: (row(b, c), r_block + 2)),
            pl.BlockSpec((CHUNK, lora_width), lambda b, c: (row(b, c), lora_block)),
            vec, vec, vec,
            pl.BlockSpec((1, lora_width), const),
            vec,
            pl.BlockSpec((lora_width, width), const),
            vec,
            pl.BlockSpec((lora_width, width), const),
            pl.BlockSpec((lora_width, width), const),
            vec, vec, vec, vec, vec,
        ],
        out_specs=pl.BlockSpec((CHUNK, width), lambda b, c: (row(b, c), 0)),
        scratch_shapes=[
            pltpu.VMEM((V7X_SUBLANES, width), F32),
            pltpu.VMEM((V7X_SUBLANES, width), F32),
            pltpu.VMEM((V7X_SUBLANES, width), F32),
            pltpu.VMEM((V7X_SUBLANES, lora_width), F32),
            pltpu.VMEM((width // V7X_LANES, V7X_LANES, V7X_LANES), F32),
        ],
        compiler_params=_cparams(("parallel", "arbitrary")),
        name="rwkv",
    )(p, p, p, p, mu_r, mu_k, mu_v, mu_l, w0, wup, a0, aup, gup, k_k, k_a, r_k, ln_w, ln_b)


def _pad_rows(w, start, total):
    return jnp.zeros((total, w.shape[1]), w.dtype).at[start:start + w.shape[0]].set(w)


def _lane_row(vals, start):
    return jnp.zeros((1, V7X_LANES), F32).at[0, start:start + vals.shape[0]].set(vals)


def kernel(x, ffn1_norm, ffn1_w_gate, ffn1_w_up, ffn1_w_down, mix_norm, w_in, gdn_conv, gdn_a_log, gdn_dt_bias, gdn_out_norm, rw_mu, rw_w0, rw_w_up, rw_a0, rw_a_up, rw_g_up, rw_k_k, rw_k_a, rw_r_k, rw_ln_w, rw_ln_b, w_branch_a, w_branch_b, w_out, ffn2_norm, ffn2_w_gate, ffn2_w_up, ffn2_w_down, final_norm):
    batch, seq, d = x.shape
    depth = w_in.shape[0]
    gdn_w = w_branch_a.shape[1]
    rw_w = w_branch_b.shape[1]
    dk = gdn_w // GDN_HEADS
    dv = gdn_out_norm.shape[1]
    hd = rw_w // RW_HEADS
    lw_, la_, lg_ = rw_w_up.shape[1], rw_a_up.shape[1], rw_g_up.shape[1]
    n_qkvz = 4 * gdn_w
    n_ab = 2 * GDN_HEADS
    n_slab = 3 * rw_w + lw_ + la_ + lg_
    lora_used = lw_ + la_ + lg_ + n_ab
    lora_width = -(-lora_used // (4 * V7X_LANES)) * (4 * V7X_LANES)
    n_mix = n_qkvz + 3 * rw_w + lora_width
    off_ab = n_qkvz
    off_slab = n_qkvz + n_ab
    off_ga = off_slab + n_slab
    a_col = n_qkvz + 3 * rw_w + lw_ + la_ + lg_
    ab_block = a_col // V7X_LANES
    a_lane = a_col % V7X_LANES
    b_lane = a_lane + GDN_HEADS

    m = batch * seq
    h = x.reshape(m, d)
    row = lambda t: t.reshape(1, -1).astype(F32)

    for l in range(depth):
        wi = w_in[l]
        w_mix = jnp.concatenate(
            [wi[:, 0:n_qkvz], wi[:, off_slab:off_slab + n_slab], wi[:, off_ab:off_ab + n_ab],
             jnp.zeros((d, lora_width - lora_used), wi.dtype)], axis=1).astype(BF16)
        wga = wi[:, off_ga:off_ga + d].astype(BF16)
        wgb = wi[:, off_ga + d:off_ga + 2 * d].astype(BF16)

        h = _ffn(h, row(ffn1_norm[l]), ffn1_w_gate[l].astype(BF16), ffn1_w_up[l].astype(BF16),
                 ffn1_w_down[l].astype(BF16), row(final_norm), final=False, tm=512, tf=512)

        p = _inproj(h, row(mix_norm[l]), w_mix, tm=512, tn=n_mix // 5)

        oa = _gdn(p, gdn_conv[l], _lane_row(gdn_a_log[l], a_lane), _lane_row(gdn_dt_bias[l], a_lane),
                  row(gdn_out_norm[l]), batch=batch, seq=seq, heads=GDN_HEADS, dk=dk, dv=dv,
                  ab_block=ab_block, a_lane=a_lane, b_lane=b_lane)

        mu = rw_mu[l]
        mu_l = jnp.zeros((1, lora_width), F32).at[0, 0:lw_ + la_ + lg_].set(mu[3 * rw_w:])
        ob = _rwkv(p, row(mu[0:rw_w]), row(mu[rw_w:2 * rw_w]), row(mu[2 * rw_w:3 * rw_w]), mu_l,
                   row(rw_w0[l]), _pad_rows(rw_w_up[l], 0, lora_width).astype(BF16),
                   row(rw_a0[l]), _pad_rows(rw_a_up[l], lw_, lora_width).astype(BF16),
                   _pad_rows(rw_g_up[l], lw_ + la_, lora_width).astype(BF16),
                   row(rw_k_k[l]), row(rw_k_a[l]), row(rw_r_k[l]), row(rw_ln_w[l]), row(rw_ln_b[l]),
                   batch=batch, seq=seq, heads=RW_HEADS, hd=hd,
                   r_block=n_qkvz // rw_w, lora_block=(n_qkvz + 3 * rw_w) // lora_width,
                   lora_width=lora_width)

        h = _mixout(h, row(mix_norm[l]), oa, ob, wga, wgb, w_branch_a[l].astype(BF16),
                    w_branch_b[l].astype(BF16), w_out[l].astype(BF16), tm=512, tj=512)

        h = _ffn(h, row(ffn2_norm[l]), ffn2_w_gate[l].astype(BF16), ffn2_w_up[l].astype(BF16),
                 ffn2_w_down[l].astype(BF16), row(final_norm), final=(l == depth - 1), tm=512, tf=512)

    return h.reshape(batch, seq, d)
```

```python
import functools

import jax
import jax.numpy as jnp
from jax import lax
from jax.experimental import pallas as pl
from jax.experimental.pallas import tpu as pltpu

F32 = jnp.float32
BF16 = jnp.bfloat16

V7X_LANES = 128
V7X_SUBLANES = 8
V7X_VMEM_BYTES = 64 * 1024 * 1024
VMEM_LIMIT_BYTES = V7X_VMEM_BYTES - 8 * 1024 * 1024

CHUNK = 64
NORM_EPS = 1e-6
L2_EPS = 1e-6
GDN_HEADS = 8
RW_HEADS = 16
RW_GN_EPS = 64e-5
SUB = 16


def _mm(a, b):
    return jnp.dot(a.astype(BF16), b.astype(BF16), preferred_element_type=F32)


def _mm_nt(a, b):
    return lax.dot_general(a.astype(BF16), b.astype(BF16), (((1,), (1,)), ((), ())),
                           preferred_element_type=F32)


def _split_bf16(x, terms):
    parts = []
    rem = x
    for _ in range(terms):
        p = rem.astype(BF16)
        parts.append(p)
        rem = rem - p.astype(F32)
    return parts


def _mm_wide_rhs(a_bf16, x, terms):
    acc = None
    for p in reversed(_split_bf16(x, terms)):
        t = jnp.dot(a_bf16, p, preferred_element_type=F32)
        acc = t if acc is None else acc + t
    return acc


def _mm_wide_lhs(x, b_bf16, terms):
    acc = None
    for p in reversed(_split_bf16(x, terms)):
        t = jnp.dot(p, b_bf16, preferred_element_type=F32)
        acc = t if acc is None else acc + t
    return acc


def _sigmoid(x):
    return jax.nn.sigmoid(x)


def _softplus(x):
    return jnp.maximum(x, 0.0) + jnp.log1p(jnp.exp(-jnp.abs(x)))


def _rmsnorm(x, gain, eps):
    ms = jnp.mean(x * x, axis=-1, keepdims=True)
    return x * lax.rsqrt(ms + eps) * gain


def _col(x, lane):
    idx = lax.broadcasted_iota(jnp.int32, x.shape, 1)
    return jnp.sum(jnp.where(idx == lane, x, 0.0), axis=-1, keepdims=True)


def _tri_masks(n):
    i = lax.broadcasted_iota(jnp.int32, (n, n), 0)
    j = lax.broadcasted_iota(jnp.int32, (n, n), 1)
    same = (i // CHUNK) == (j // CHUNK)
    incl = same & (j <= i)
    strict = same & (j < i)
    eye = jnp.where(i == j, 1.0, 0.0).astype(F32)
    lvl0 = (i // SUB) == (j // SUB)
    levels = []
    size = SUB
    while size < CHUNK:
        levels.append(((i // (2 * size)) == (j // (2 * size))) & ((i // size) != (j // size)))
        size *= 2
    return incl, strict, eye, lvl0, levels


def _unit_lower_inverse(mats, eye, lvl0, levels):
    pws = [jnp.where(lvl0, a, 0.0) for a in mats]
    ts = [eye - d for d in pws]
    span = 2
    while span < SUB:
        pws = [_mm(pw, pw) for pw in pws]
        ts = [t + _mm(t, pw) for t, pw in zip(ts, pws)]
        span *= 2
    for m in levels:
        tes = [_mm(t, jnp.where(m, a, 0.0)) for t, a in zip(ts, mats)]
        ts = [t - _mm(te, t) for t, te in zip(ts, tes)]
    return ts


ROW_BLOCK = 256


def _for_row_blocks(rows, body):
    step = min(ROW_BLOCK, rows)

    def step_fn(i, carry):
        body(pl.ds(pl.multiple_of(i * step, step), step))
        return carry

    lax.fori_loop(0, rows // step, step_fn, 0)


def _norm_rows(x_ref, gain_ref, xn_ref):
    def body(rows):
        xn_ref[rows, :] = _rmsnorm(x_ref[rows, :], gain_ref[...], NORM_EPS).astype(BF16)

    _for_row_blocks(x_ref.shape[0], body)


def _cparams(sem):
    return pltpu.CompilerParams(dimension_semantics=sem, vmem_limit_bytes=VMEM_LIMIT_BYTES)


def _ffn_kernel(x_ref, gain_ref, wg_ref, wu_ref, wd_ref, fgain_ref, o_ref, xn_ref, *, nj, final):
    j = pl.program_id(1)

    @pl.when(j == 0)
    def _():
        _norm_rows(x_ref, gain_ref, xn_ref)
        o_ref[...] = jnp.zeros_like(o_ref)

    xn = xn_ref[...]
    g = jnp.dot(xn, wg_ref[...], preferred_element_type=F32)
    u = jnp.dot(xn, wu_ref[...], preferred_element_type=F32)
    act = (g * _sigmoid(g) * u).astype(BF16)
    o_ref[...] += jnp.dot(act, wd_ref[...], preferred_element_type=F32)

    @pl.when(j == nj - 1)
    def _():
        def body(rows):
            hn = x_ref[rows, :] + 0.5 * o_ref[rows, :]
            if final:
                hn = _rmsnorm(hn, fgain_ref[...], NORM_EPS)
            o_ref[rows, :] = hn

        _for_row_blocks(o_ref.shape[0], body)


def _ffn(h, gain, wg, wu, wd, fgain, *, final, tm, tf):
    m, d = h.shape
    f = wg.shape[1]
    nj = f // tf
    return pl.pallas_call(
        functools.partial(_ffn_kernel, nj=nj, final=final),
        out_shape=jax.ShapeDtypeStruct((m, d), F32),
        grid=(m // tm, nj),
        in_specs=[
            pl.BlockSpec((tm, d), lambda i, j: (i, 0), pipeline_mode=pl.Buffered(1)),
            pl.BlockSpec((1, d), lambda i, j: (0, 0)),
            pl.BlockSpec((d, tf), lambda i, j: (0, j)),
            pl.BlockSpec((d, tf), lambda i, j: (0, j)),
            pl.BlockSpec((tf, d), lambda i, j: (j, 0)),
            pl.BlockSpec((1, d), lambda i, j: (0, 0)),
        ],
        out_specs=pl.BlockSpec((tm, d), lambda i, j: (i, 0)),
        scratch_shapes=[pltpu.VMEM((tm, d), BF16)],
        compiler_params=_cparams(("parallel", "arbitrary")),
        name="ffn",
    )(h, gain, wg, wu, wd, fgain)


def _inproj_kernel(x_ref, gain_ref, w_ref, o_ref, xn_ref):
    @pl.when(pl.program_id(1) == 0)
    def _():
        _norm_rows(x_ref, gain_ref, xn_ref)

    o_ref[...] = jnp.dot(xn_ref[...], w_ref[...], preferred_element_type=F32)


def _inproj(h, gain, w, *, tm, tn):
    m, d = h.shape
    n = w.shape[1]
    return pl.pallas_call(
        _inproj_kernel,
        out_shape=jax.ShapeDtypeStruct((m, n), F32),
        grid=(m // tm, n // tn),
        in_specs=[
            pl.BlockSpec((tm, d), lambda i, j: (i, 0)),
            pl.BlockSpec((1, d), lambda i, j: (0, 0)),
            pl.BlockSpec((d, tn), lambda i, j: (0, j)),
        ],
        out_specs=pl.BlockSpec((tm, tn), lambda i, j: (i, j)),
        scratch_shapes=[pltpu.VMEM((tm, d), BF16)],
        compiler_params=_cparams(("parallel", "arbitrary")),
        name="inproj",
    )(h, gain, w)


def _mixout_kernel(x_ref, gain_ref, oa_ref, ob_ref, wga_ref, wgb_ref, wba_ref, wbb_ref, wo_ref,
                   o_ref, xn_ref, *, nj):
    j = pl.program_id(1)

    @pl.when(j == 0)
    def _():
        _norm_rows(x_ref, gain_ref, xn_ref)
        o_ref[...] = jnp.zeros_like(o_ref)

    xn = xn_ref[...]
    ga = jnp.dot(xn, wga_ref[...], preferred_element_type=F32)
    gb = jnp.dot(xn, wgb_ref[...], preferred_element_type=F32)
    ya = jnp.dot(oa_ref[...], wba_ref[...], preferred_element_type=F32)
    yb = jnp.dot(ob_ref[...], wbb_ref[...], preferred_element_type=F32)
    y = (_sigmoid(ga) * ya + _sigmoid(gb) * yb).astype(BF16)
    o_ref[...] += jnp.dot(y, wo_ref[...], preferred_element_type=F32)

    @pl.when(j == nj - 1)
    def _():
        def body(rows):
            o_ref[rows, :] = x_ref[rows, :] + o_ref[rows, :]

        _for_row_blocks(o_ref.shape[0], body)


def _mixout(h, gain, oa, ob, wga, wgb, wba, wbb, wo, *, tm, tj):
    m, d = h.shape
    na = oa.shape[1]
    nb = ob.shape[1]
    dy = wga.shape[1]
    nj = dy // tj
    return pl.pallas_call(
        functools.partial(_mixout_kernel, nj=nj),
        out_shape=jax.ShapeDtypeStruct((m, d), F32),
        grid=(m // tm, nj),
        in_specs=[
            pl.BlockSpec((tm, d), lambda i, j: (i, 0)),
            pl.BlockSpec((1, d), lambda i, j: (0, 0)),
            pl.BlockSpec((tm, na), lambda i, j: (i, 0)),
            pl.BlockSpec((tm, nb), lambda i, j: (i, 0)),
            pl.BlockSpec((d, tj), lambda i, j: (0, j)),
            pl.BlockSpec((d, tj), lambda i, j: (0, j)),
            pl.BlockSpec((na, tj), lambda i, j: (0, j)),
            pl.BlockSpec((nb, tj), lambda i, j: (0, j)),
            pl.BlockSpec((tj, d), lambda i, j: (j, 0)),
        ],
        out_specs=pl.BlockSpec((tm, d), lambda i, j: (i, 0)),
        scratch_shapes=[pltpu.VMEM((tm, d), BF16)],
        compiler_params=_cparams(("parallel", "arbitrary")),
        name="mixout",
    )(h, gain, oa, ob, wga, wgb, wba, wbb, wo)


def _gdn_kernel(qkvz_ref, ab_ref, convw_ref, alog_ref, dtb_ref, ogain_ref, o_ref, xpad_ref, state_ref,
                *, heads, dk, dv, taps, a_lane, b_lane):
    C = CHUNK
    PAD = V7X_SUBLANES
    nq = heads * dk
    nqkv = 2 * nq + heads * dv

    @pl.when(pl.program_id(1) == 0)
    def _():
        xpad_ref[0:PAD, :] = jnp.zeros((PAD, nqkv), F32)
        state_ref[...] = jnp.zeros_like(state_ref)

    x = qkvz_ref[:, 0:nqkv]
    xpad_ref[PAD:PAD + C, :] = x
    acc = None
    for j in range(taps):
        term = xpad_ref[pl.ds(PAD - (taps - 1) + j, C), :] * convw_ref[j:j + 1, :]
        acc = term if acc is None else acc + term
    xpad_ref[0:PAD, :] = x[C - PAD:C, :]
    y = acc * _sigmoid(acc)

    ab = ab_ref[...]
    g = -jnp.exp(alog_ref[...]) * _softplus(ab + dtb_ref[...])
    beta = _sigmoid(ab)
    incl, strict, eye, lvl0, levels = _tri_masks(C)
    tril = jnp.where(incl, 1.0, 0.0).astype(BF16)
    gc = _mm_wide_rhs(tril, g, 3)
    glast = gc[C - 1:C, :]
    e_pos = jnp.exp(gc)
    e_rem = jnp.exp(glast - gc)
    e_last = jnp.exp(glast)
    gc_t = gc.T

    hs = range(heads)
    qs, ks, vs, ecols, rcols, elasts, bcols, gammas = [], [], [], [], [], [], [], []
    for h in hs:
        qh = y[:, h * dk:(h + 1) * dk]
        kh = y[:, nq + h * dk:nq + (h + 1) * dk]
        qs.append(qh * lax.rsqrt(jnp.sum(qh * qh, axis=-1, keepdims=True) + L2_EPS) * (dk ** -0.5))
        ks.append(kh * lax.rsqrt(jnp.sum(kh * kh, axis=-1, keepdims=True) + L2_EPS))
        vs.append(y[:, 2 * nq + h * dv:2 * nq + (h + 1) * dv])
        la = a_lane + h
        ecols.append(_col(e_pos, la))
        rcols.append(_col(e_rem, la))
        elasts.append(_col(e_last, la))
        bcols.append(_col(beta, b_lane + h))
        diff = _col(gc, la) - gc_t[la:la + 1, :]
        gammas.append(jnp.where(incl, jnp.exp(jnp.where(incl, diff, 0.0)), 0.0))
    kbs = [ks[h] * bcols[h] for h in hs]
    scs = [_mm_nt(jnp.concatenate([kbs[h], qs[h]], axis=0), ks[h]) for h in hs]
    ts = _unit_lower_inverse([jnp.where(strict, scs[h][0:C] * gammas[h], 0.0) for h in hs], eye, lvl0, levels)
    uws = [_mm(ts[h], jnp.concatenate([vs[h] * bcols[h], kbs[h] * ecols[h]], axis=1)) for h in hs]
    ss = [state_ref[h] for h in hs]
    wss = [_mm(jnp.concatenate([uws[h][:, dv:dv + dk], qs[h] * ecols[h]], axis=0), ss[h]) for h in hs]
    vnews = [uws[h][:, 0:dv] - wss[h][0:C] for h in hs]
    os_ = [wss[h][C:2 * C] + _mm(scs[h][C:2 * C] * gammas[h], vnews[h]) for h in hs]
    for h in hs:
        state_ref[h] = ss[h] * elasts[h] + _mm((ks[h] * rcols[h]).T, vnews[h])
    for h in hs:
        zh = qkvz_ref[:, nqkv + h * dv:nqkv + (h + 1) * dv]
        on = _rmsnorm(os_[h], ogain_ref[...], NORM_EPS)
        o_ref[:, h * dv:(h + 1) * dv] = (on * (zh * _sigmoid(zh))).astype(o_ref.dtype)


def _gdn(p, convw, alog_row, dtb_row, ogain, *, batch, seq, heads, dk, dv, ab_block, a_lane, b_lane):
    nc = seq // CHUNK
    nqkvz = 2 * heads * dk + 2 * heads * dv
    nqkv = nqkvz - heads * dv
    taps = convw.shape[0]
    return pl.pallas_call(
        functools.partial(_gdn_kernel, heads=heads, dk=dk, dv=dv, taps=taps, a_lane=a_lane, b_lane=b_lane),
        out_shape=jax.ShapeDtypeStruct((batch * seq, heads * dv), BF16),
        grid=(batch, nc),
        in_specs=[
            pl.BlockSpec((CHUNK, nqkvz), lambda b, c: (b * nc + c, 0)),
            pl.BlockSpec((CHUNK, V7X_LANES), lambda b, c: (b * nc + c, ab_block)),
            pl.BlockSpec((taps, nqkv), lambda b, c: (0, 0)),
            pl.BlockSpec((1, V7X_LANES), lambda b, c: (0, 0)),
            pl.BlockSpec((1, V7X_LANES), lambda b, c: (0, 0)),
            pl.BlockSpec((1, dv), lambda b, c: (0, 0)),
        ],
        out_specs=pl.BlockSpec((CHUNK, heads * dv), lambda b, c: (b * nc + c, 0)),
        scratch_shapes=[
            pltpu.VMEM((CHUNK + V7X_SUBLANES, nqkv), F32),
            pltpu.VMEM((heads, dk, dv), F32),
        ],
        compiler_params=_cparams(("parallel", "arbitrary")),
        name="gdn",
    )(p, p, convw, alog_row, dtb_row, ogain)


def _rwkv_kernel(r_ref, k_ref, v_ref, l_ref, mur_ref, muk_ref, muv_ref, mul_ref, w0_ref, wup_ref, a0_ref,
                 aup_ref, gup_ref, kk_ref, ka_ref, rk_ref, lnw_ref, lnb_ref, o_ref,
                 pr_ref, pk_ref, pv_ref, pl_ref, state_ref, *, heads, hd):
    C = CHUNK
    PAD = V7X_SUBLANES
    LW = V7X_LANES
    npairs = heads * hd // LW
    first = pl.program_id(1) == 0

    @pl.when(first)
    def _():
        for ref in (pr_ref, pk_ref, pv_ref, pl_ref):
            ref[...] = jnp.zeros_like(ref)
        state_ref[...] = jnp.zeros_like(state_ref)

    def shift_mix(x_ref, prev_ref, mu_ref):
        x = x_ref[...]
        row = lax.broadcasted_iota(jnp.int32, x.shape, 0)
        xprev = jnp.where(row == 0, prev_ref[PAD - 1:PAD, :], pltpu.roll(x, 1, 0))
        prev_ref[...] = x[C - PAD:C, :]
        return x + (xprev - x) * mu_ref[...]

    r = shift_mix(r_ref, pr_ref, mur_ref)
    k = shift_mix(k_ref, pk_ref, muk_ref)
    v = shift_mix(v_ref, pv_ref, muv_ref)
    lo = shift_mix(l_ref, pl_ref, mul_ref)

    wpre = w0_ref[...] + _mm(jnp.tanh(lo), wup_ref[...])
    lw = -jnp.exp(-_softplus(-wpre) - 0.5)
    a = _sigmoid(a0_ref[...] + _mm(lo, aup_ref[...]))
    gate = _mm(_sigmoid(lo), gup_ref[...])

    incl, strict, eye, lvl0, levels = _tri_masks(2 * C)
    li = lax.broadcasted_iota(jnp.int32, (LW, LW), 0)
    lj = lax.broadcasted_iota(jnp.int32, (LW, LW), 1)
    headsum = jnp.where((li // hd) == (lj // hd), 1.0, 0.0).astype(BF16)
    ci = lax.broadcasted_iota(jnp.int32, (C, C), 0)
    cj = lax.broadcasted_iota(jnp.int32, (C, C), 1)
    tril = jnp.where(cj <= ci, 1.0, 0.0).astype(BF16)
    lane = lax.broadcasted_iota(jnp.int32, (C, LW), 1)
    m0 = lane < hd

    kkraw = k * kk_ref[...]
    k2 = k * (1.0 + (a - 1.0) * ka_ref[...])
    cum = _mm_wide_rhs(tril, lw, 3)
    clast = cum[C - 1:C, :]
    e_pos = jnp.exp(cum)
    e_neg = jnp.exp(-cum)
    e_exc = jnp.exp(cum - lw)
    e_rem = jnp.exp(clast - cum)
    wc = jnp.exp(clast)
    rkr = r * k2 * rk_ref[...]

    def stack(x):
        return jnp.concatenate([jnp.where(m0, x, 0.0), jnp.where(m0, 0.0, x)], axis=0)

    ps = range(npairs)
    sls = [slice(p * LW, (p + 1) * LW) for p in ps]
    sums = [_mm_wide_lhs(jnp.concatenate([kkraw[:, sl] * kkraw[:, sl], rkr[:, sl]], axis=0), headsum, 2)
            for sl in sls]
    kks = [kkraw[:, sl] * lax.rsqrt(sums[p][0:C] + L2_EPS) for p, sl in enumerate(sls)]
    kkas = [kks[p] * a[:, sl] for p, sl in enumerate(sls)]
    a_ss = [stack(-kks[p] * e_exc[:, sl]) for p, sl in enumerate(sls)]
    r_ss = [stack(r[:, sl] * e_pos[:, sl]) for sl in sls]
    b_ss = [stack(kkas[p] * e_neg[:, sl]) for p, sl in enumerate(sls)]
    k_ss = [stack(k2[:, sl] * e_neg[:, sl]) for sl in sls]
    v_ss = [stack(v[:, sl]) for sl in sls]
    scs = [_mm_nt(jnp.concatenate([a_ss[p], r_ss[p]], axis=0), jnp.concatenate([b_ss[p], k_ss[p]], axis=0))
           for p in ps]
    lvs = [_mm(jnp.where(strict, scs[p][0:2 * C, 2 * C:4 * C], 0.0), v_ss[p]) for p in ps]
    ts = _unit_lower_inverse([jnp.where(strict, -scs[p][0:2 * C, 0:2 * C], 0.0) for p in ps], eye, lvl0, levels)
    tts = [_mm(ts[p], jnp.concatenate([a_ss[p], lvs[p]], axis=1)) for p in ps]
    qs = [state_ref[p] for p in ps]
    pqs = [_mm_nt(jnp.concatenate([tts[p][:, 0:LW], r_ss[p]], axis=0), qs[p]) for p in ps]
    uvs = [jnp.concatenate([pqs[p][0:2 * C] + tts[p][:, LW:2 * LW], v_ss[p]], axis=0) for p in ps]
    for p, sl in enumerate(sls):
        bk = jnp.concatenate([stack(kkas[p] * e_rem[:, sl]), stack(k2[:, sl] * e_rem[:, sl])], axis=0)
        state_ref[p] = qs[p] * wc[:, sl] + _mm(uvs[p].T, bk)
    y_ss = [pqs[p][2 * C:4 * C]
            + _mm(jnp.concatenate([jnp.where(incl, scs[p][2 * C:4 * C, 0:2 * C], 0.0),
                                   jnp.where(incl, scs[p][2 * C:4 * C, 2 * C:4 * C], 0.0)], axis=1), uvs[p])
            for p in ps]
    yps = [y_s[0:C] + y_s[C:2 * C] for y_s in y_ss]

    dlts = [yp - _mm_wide_lhs(yp, headsum, 2) * (1.0 / hd) for yp in yps]
    vars_ = [_mm_wide_lhs(dlt * dlt, headsum, 2) * (1.0 / hd) for dlt in dlts]
    for p, sl in enumerate(sls):
        yn = dlts[p] * lax.rsqrt(vars_[p] + RW_GN_EPS) * lnw_ref[:, sl] + lnb_ref[:, sl]
        bonus = sums[p][C:2 * C] * v[:, sl]
        o_ref[:, sl] = ((yn + bonus) * gate[:, sl]).astype(o_ref.dtype)


def _rwkv(p, mu_r, mu_k, mu_v, mu_l, w0, wup, a0, aup, gup, k_k, k_a, r_k, ln_w, ln_b,
          *, batch, seq, heads, hd, r_block, lora_block, lora_width):
    nc = seq // CHUNK
    width = heads * hd
    row = lambda b, c: b * nc + c
    const = lambda b, c: (0, 0)
    vec = pl.BlockSpec((1, width), const)
    return pl.pallas_call(
        functools.partial(_rwkv_kernel, heads=heads, hd=hd),
        out_shape=jax.ShapeDtypeStruct((batch * seq, width), BF16),
        grid=(batch, nc),
        in_specs=[
            pl.BlockSpec((CHUNK, width), lambda b, c: (row(b, c), r_block)),
            pl.BlockSpec((CHUNK, width), lambda b, c: (row(b, c), r_block + 1)),
            pl.BlockSpec((CHUNK, width), lambda b, c: (row(b, c), r_block + 2)),
            pl.BlockSpec((CHUNK, lora_width), lambda b, c: (row(b, c), lora_block)),
            vec, vec, vec,
            pl.BlockSpec((1, lora_width), const),
            vec,
            pl.BlockSpec((lora_width, width), const),
            vec,
            pl.BlockSpec((lora_width, width), const),
            pl.BlockSpec((lora_width, width), const),
            vec, vec, vec, vec, vec,
        ],
        out_specs=pl.BlockSpec((CHUNK, width), lambda b, c: (row(b, c), 0)),
        scratch_shapes=[
            pltpu.VMEM((V7X_SUBLANES, width), F32),
            pltpu.VMEM((V7X_SUBLANES, width), F32),
            pltpu.VMEM((V7X_SUBLANES, width), F32),
            pltpu.VMEM((V7X_SUBLANES, lora_width), F32),
            pltpu.VMEM((width // V7X_LANES, V7X_LANES, V7X_LANES), F32),
        ],
        compiler_params=_cparams(("parallel", "arbitrary")),
        name="rwkv",
    )(p, p, p, p, mu_r, mu_k, mu_v, mu_l, w0, wup, a0, aup, gup, k_k, k_a, r_k, ln_w, ln_b)


def _pad_rows(w, start, total):
    return jnp.zeros((total, w.shape[1]), w.dtype).at[start:start + w.shape[0]].set(w)


def _lane_row(vals, start):
    return jnp.zeros((1, V7X_LANES), F32).at[0, start:start + vals.shape[0]].set(vals)


def kernel(x, ffn1_norm, ffn1_w_gate, ffn1_w_up, ffn1_w_down, mix_norm, w_in, gdn_conv, gdn_a_log, gdn_dt_bias, gdn_out_norm, rw_mu, rw_w0, rw_w_up, rw_a0, rw_a_up, rw_g_up, rw_k_k, rw_k_a, rw_r_k, rw_ln_w, rw_ln_b, w_branch_a, w_branch_b, w_out, ffn2_norm, ffn2_w_gate, ffn2_w_up, ffn2_w_down, final_norm):
    batch, seq, d = x.shape
    depth = w_in.shape[0]
    gdn_w = w_branch_a.shape[1]
    rw_w = w_branch_b.shape[1]
    dk = gdn_w // GDN_HEADS
    dv = gdn_out_norm.shape[1]
    hd = rw_w // RW_HEADS
    lw_, la_, lg_ = rw_w_up.shape[1], rw_a_up.shape[1], rw_g_up.shape[1]
    n_qkvz = 4 * gdn_w
    n_ab = 2 * GDN_HEADS
    n_slab = 3 * rw_w + lw_ + la_ + lg_
    lora_used = lw_ + la_ + lg_ + n_ab
    lora_width = -(-lora_used // (4 * V7X_LANES)) * (4 * V7X_LANES)
    n_mix = n_qkvz + 3 * rw_w + lora_width
    off_ab = n_qkvz
    off_slab = n_qkvz + n_ab
    off_ga = off_slab + n_slab
    a_col = n_qkvz + 3 * rw_w + lw_ + la_ + lg_
    ab_block = a_col // V7X_LANES
    a_lane = a_col % V7X_LANES
    b_lane = a_lane + GDN_HEADS

    m = batch * seq
    h = x.reshape(m, d)
    row = lambda t: t.reshape(1, -1).astype(F32)

    for l in range(depth):
        wi = w_in[l]
        w_mix = jnp.concatenate(
            [wi[:, 0:n_qkvz], wi[:, off_slab:off_slab + n_slab], wi[:, off_ab:off_ab + n_ab],
             jnp.zeros((d, lora_width - lora_used), wi.dtype)], axis=1).astype(BF16)
        wga = wi[:, off_ga:off_ga + d].astype(BF16)
        wgb = wi[:, off_ga + d:off_ga + 2 * d].astype(BF16)

        h = _ffn(h, row(ffn1_norm[l]), ffn1_w_gate[l].astype(BF16), ffn1_w_up[l].astype(BF16),
                 ffn1_w_down[l].astype(BF16), row(final_norm), final=False, tm=1024, tf=512)

        p = _inproj(h, row(mix_norm[l]), w_mix, tm=1024, tn=n_mix // 5)

        oa = _gdn(p, gdn_conv[l], _lane_row(gdn_a_log[l], a_lane), _lane_row(gdn_dt_bias[l], a_lane),
                  row(gdn_out_norm[l]), batch=batch, seq=seq, heads=GDN_HEADS, dk=dk, dv=dv,
                  ab_block=ab_block, a_lane=a_lane, b_lane=b_lane)

        mu = rw_mu[l]
        mu_l = jnp.zeros((1, lora_width), F32).at[0, 0:lw_ + la_ + lg_].set(mu[3 * rw_w:])
        ob = _rwkv(p, row(mu[0:rw_w]), row(mu[rw_w:2 * rw_w]), row(mu[2 * rw_w:3 * rw_w]), mu_l,
                   row(rw_w0[l]), _pad_rows(rw_w_up[l], 0, lora_width).astype(BF16),
                   row(rw_a0[l]), _pad_rows(rw_a_up[l], lw_, lora_width).astype(BF16),
                   _pad_rows(rw_g_up[l], lw_ + la_, lora_width).astype(BF16),
                   row(rw_k_k[l]), row(rw_k_a[l]), row(rw_r_k[l]), row(rw_ln_w[l]), row(rw_ln_b[l]),
                   batch=batch, seq=seq, heads=RW_HEADS, hd=hd,
                   r_block=n_qkvz // rw_w, lora_block=(n_qkvz + 3 * rw_w) // lora_width,
                   lora_width=lora_width)

        h = _mixout(h, row(mix_norm[l]), oa, ob, wga, wgb, w_branch_a[l].astype(BF16),
                    w_branch_b[l].astype(BF16), w_out[l].astype(BF16), tm=512, tj=512)

        h = _ffn(h, row(ffn2_norm[l]), ffn2_w_gate[l].astype(BF16), ffn2_w_up[l].astype(BF16),
                 ffn2_w_down[l].astype(BF16), row(final_norm), final=(l == depth - 1), tm=1024, tf=512)

    return h.reshape(batch, seq, d)
```

```python
import functools

import jax
import jax.numpy as jnp
from jax import lax
from jax.experimental import pallas as pl
from jax.experimental.pallas import tpu as pltpu

F32 = jnp.float32
BF16 = jnp.bfloat16

V7X_LANES = 128
V7X_SUBLANES = 8
V7X_VMEM_BYTES = 64 * 1024 * 1024
VMEM_LIMIT_BYTES = V7X_VMEM_BYTES - 8 * 1024 * 1024

CHUNK = 64
NORM_EPS = 1e-6
L2_EPS = 1e-6
GDN_HEADS = 8
RW_HEADS = 16
RW_GN_EPS = 64e-5
SUB = 16
ROW_BLOCK = 256


def _mm(a, b):
    return jnp.dot(a.astype(BF16), b.astype(BF16), preferred_element_type=F32)


def _mm_nt(a, b):
    return lax.dot_general(a.astype(BF16), b.astype(BF16), (((1,), (1,)), ((), ())),
                           preferred_element_type=F32)


def _split_bf16(x, terms):
    parts = []
    rem = x
    for _ in range(terms):
        p = rem.astype(BF16)
        parts.append(p)
        rem = rem - p.astype(F32)
    return parts


def _mm_wide_rhs(a_bf16, x, terms):
    acc = None
    for p in reversed(_split_bf16(x, terms)):
        t = jnp.dot(a_bf16, p, preferred_element_type=F32)
        acc = t if acc is None else acc + t
    return acc


def _mm_wide_lhs(x, b_bf16, terms):
    acc = None
    for p in reversed(_split_bf16(x, terms)):
        t = jnp.dot(p, b_bf16, preferred_element_type=F32)
        acc = t if acc is None else acc + t
    return acc


def _sigmoid(x):
    return jax.nn.sigmoid(x)


def _softplus(x):
    return jnp.maximum(x, 0.0) + jnp.log1p(jnp.exp(-jnp.abs(x)))


def _rmsnorm(x, gain, eps):
    ms = jnp.mean(x * x, axis=-1, keepdims=True)
    return x * lax.rsqrt(ms + eps) * gain


def _col(x, lane):
    idx = lax.broadcasted_iota(jnp.int32, x.shape, 1)
    return jnp.sum(jnp.where(idx == lane, x, 0.0), axis=-1, keepdims=True)


def _tri_masks(n):
    i = lax.broadcasted_iota(jnp.int32, (n, n), 0)
    j = lax.broadcasted_iota(jnp.int32, (n, n), 1)
    same = (i // CHUNK) == (j // CHUNK)
    incl = same & (j <= i)
    strict = same & (j < i)
    eye = jnp.where(i == j, 1.0, 0.0).astype(F32)
    lvl0 = (i // SUB) == (j // SUB)
    levels = []
    size = SUB
    while size < CHUNK:
        levels.append(((i // (2 * size)) == (j // (2 * size))) & ((i // size) != (j // size)))
        size *= 2
    return incl, strict, eye, lvl0, levels


def _unit_lower_inverse(mats, eye, lvl0, levels):
    pws = [jnp.where(lvl0, a, 0.0) for a in mats]
    ts = [eye - d for d in pws]
    span = 2
    while span < SUB:
        pws = [_mm(pw, pw) for pw in pws]
        ts = [t + _mm(t, pw) for t, pw in zip(ts, pws)]
        span *= 2
    for m in levels:
        tes = [_mm(t, jnp.where(m, a, 0.0)) for t, a in zip(ts, mats)]
        ts = [t - _mm(te, t) for t, te in zip(ts, tes)]
    return ts


def _for_row_blocks(rows, body):
    step = min(ROW_BLOCK, rows)

    def step_fn(i, carry):
        body(pl.ds(pl.multiple_of(i * step, step), step))
        return carry

    lax.fori_loop(0, rows // step, step_fn, 0)


def _cparams(sem):
    return pltpu.CompilerParams(dimension_semantics=sem, vmem_limit_bytes=VMEM_LIMIT_BYTES)


def _ffn_kernel(x_ref, gain_ref, wg_ref, wu_ref, wd_ref, fgain_ref, o_ref, xn_ref, *, nj, final):
    j = pl.program_id(1)

    @pl.when(j == 0)
    def _():
        def body(rows):
            x = x_ref[rows, :]
            xn_ref[rows, :] = _rmsnorm(x, gain_ref[...], NORM_EPS).astype(BF16)
            o_ref[rows, :] = x

        _for_row_blocks(x_ref.shape[0], body)

    xn = xn_ref[...]
    g = jnp.dot(xn, wg_ref[...], preferred_element_type=F32)
    u = jnp.dot(xn, wu_ref[...], preferred_element_type=F32)
    act = (0.5 * (g * _sigmoid(g) * u)).astype(BF16)
    o_ref[...] += jnp.dot(act, wd_ref[...], preferred_element_type=F32)

    if final:
        @pl.when(j == nj - 1)
        def _():
            def body(rows):
                o_ref[rows, :] = _rmsnorm(o_ref[rows, :], fgain_ref[...], NORM_EPS)

            _for_row_blocks(o_ref.shape[0], body)


def _ffn(h, gain, wg, wu, wd, fgain, *, final, tm, tf):
    m, d = h.shape
    f = wg.shape[1]
    nj = f // tf
    return pl.pallas_call(
        functools.partial(_ffn_kernel, nj=nj, final=final),
        out_shape=jax.ShapeDtypeStruct((m, d), F32),
        grid=(m // tm, nj),
        in_specs=[
            pl.BlockSpec((tm, d), lambda i, j: (i, 0)),
            pl.BlockSpec((1, d), lambda i, j: (0, 0)),
            pl.BlockSpec((d, tf), lambda i, j: (0, j)),
            pl.BlockSpec((d, tf), lambda i, j: (0, j)),
            pl.BlockSpec((tf, d), lambda i, j: (j, 0)),
            pl.BlockSpec((1, d), lambda i, j: (0, 0)),
        ],
        out_specs=pl.BlockSpec((tm, d), lambda i, j: (i, 0)),
        scratch_shapes=[pltpu.VMEM((tm, d), BF16)],
        compiler_params=_cparams(("parallel", "arbitrary")),
        name="ffn",
    )(h, gain, wg, wu, wd, fgain)


def _inproj_kernel(x_ref, gain_ref, w_ref, o_ref, xn_ref):
    @pl.when(pl.program_id(1) == 0)
    def _():
        def body(rows):
            xn_ref[rows, :] = _rmsnorm(x_ref[rows, :], gain_ref[...], NORM_EPS).astype(BF16)

        _for_row_blocks(x_ref.shape[0], body)

    o_ref[...] = jnp.dot(xn_ref[...], w_ref[...], preferred_element_type=F32)


def _inproj(h, gain, w, *, tm, tn):
    m, d = h.shape
    n = w.shape[1]
    return pl.pallas_call(
        _inproj_kernel,
        out_shape=jax.ShapeDtypeStruct((m, n), F32),
        grid=(m // tm, n // tn),
        in_specs=[
            pl.BlockSpec((tm, d), lambda i, j: (i, 0)),
            pl.BlockSpec((1, d), lambda i, j: (0, 0)),
            pl.BlockSpec((d, tn), lambda i, j: (0, j)),
        ],
        out_specs=pl.BlockSpec((tm, tn), lambda i, j: (i, j)),
        scratch_shapes=[pltpu.VMEM((tm, d), BF16)],
        compiler_params=_cparams(("parallel", "arbitrary")),
        name="inproj",
    )(h, gain, w)


def _mixout_kernel(x_ref, gain_ref, oa_ref, ob_ref, wga_ref, wgb_ref, wba_ref, wbb_ref, wo_ref,
                   o_ref, xn_ref):
    @pl.when(pl.program_id(1) == 0)
    def _():
        def body(rows):
            x = x_ref[rows, :]
            xn_ref[rows, :] = _rmsnorm(x, gain_ref[...], NORM_EPS).astype(BF16)
            o_ref[rows, :] = x

        _for_row_blocks(x_ref.shape[0], body)

    xn = xn_ref[...]
    ga = jnp.dot(xn, wga_ref[...], preferred_element_type=F32)
    gb = jnp.dot(xn, wgb_ref[...], preferred_element_type=F32)
    ya = jnp.dot(oa_ref[...], wba_ref[...], preferred_element_type=F32)
    yb = jnp.dot(ob_ref[...], wbb_ref[...], preferred_element_type=F32)
    y = (_sigmoid(ga) * ya + _sigmoid(gb) * yb).astype(BF16)
    o_ref[...] += jnp.dot(y, wo_ref[...], preferred_element_type=F32)


def _mixout(h, gain, oa, ob, wga, wgb, wba, wbb, wo, *, tm, tj):
    m, d = h.shape
    na = oa.shape[1]
    nb = ob.shape[1]
    dy = wga.shape[1]
    return pl.pallas_call(
        _mixout_kernel,
        out_shape=jax.ShapeDtypeStruct((m, d), F32),
        grid=(m // tm, dy // tj),
        in_specs=[
            pl.BlockSpec((tm, d), lambda i, j: (i, 0)),
            pl.BlockSpec((1, d), lambda i, j: (0, 0)),
            pl.BlockSpec((tm, na), lambda i, j: (i, 0)),
            pl.BlockSpec((tm, nb), lambda i, j: (i, 0)),
            pl.BlockSpec((d, tj), lambda i, j: (0, j)),
            pl.BlockSpec((d, tj), lambda i, j: (0, j)),
            pl.BlockSpec((na, tj), lambda i, j: (0, j)),
            pl.BlockSpec((nb, tj), lambda i, j: (0, j)),
            pl.BlockSpec((tj, d), lambda i, j: (j, 0)),
        ],
        out_specs=pl.BlockSpec((tm, d), lambda i, j: (i, 0)),
        scratch_shapes=[pltpu.VMEM((tm, d), BF16)],
        compiler_params=_cparams(("parallel", "arbitrary")),
        name="mixout",
    )(h, gain, oa, ob, wga, wgb, wba, wbb, wo)


def _gdn_kernel(qkvz_ref, ab_ref, convw_ref, alog_ref, dtb_ref, ogain_ref, o_ref, xpad_ref, state_ref,
                *, batch, heads, dk, dv, taps, a_lane, b_lane):
    C = CHUNK
    PAD = V7X_SUBLANES
    nq = heads * dk
    nqkv = 2 * nq + heads * dv

    @pl.when(pl.program_id(0) == 0)
    def _():
        for b in range(batch):
            xpad_ref[b, 0:PAD, :] = jnp.zeros((PAD, nqkv), F32)
        state_ref[...] = jnp.zeros_like(state_ref)

    incl, strict, eye, lvl0, levels = _tri_masks(C)
    tril = jnp.where(incl, 1.0, 0.0).astype(BF16)

    units = []
    qs, ks, vs, ecols, rcols, elasts, bcols, gammas = [], [], [], [], [], [], [], []
    for b in range(batch):
        x = qkvz_ref[b, :, 0:nqkv]
        xpad_ref[b, PAD:PAD + C, :] = x
        acc = None
        for j in range(taps):
            term = xpad_ref[b, pl.ds(PAD - (taps - 1) + j, C), :] * convw_ref[j:j + 1, :]
            acc = term if acc is None else acc + term
        xpad_ref[b, 0:PAD, :] = x[C - PAD:C, :]
        y = acc * _sigmoid(acc)

        ab = ab_ref[b]
        g = -jnp.exp(alog_ref[...]) * _softplus(ab + dtb_ref[...])
        beta = _sigmoid(ab)
        gc = _mm_wide_rhs(tril, g, 3)
        glast = gc[C - 1:C, :]
        e_pos = jnp.exp(gc)
        e_rem = jnp.exp(glast - gc)
        e_last = jnp.exp(glast)
        gc_t = gc.T
        for h in range(heads):
            units.append((b, h))
            qh = y[:, h * dk:(h + 1) * dk]
            kh = y[:, nq + h * dk:nq + (h + 1) * dk]
            qs.append(qh * lax.rsqrt(jnp.sum(qh * qh, axis=-1, keepdims=True) + L2_EPS) * (dk ** -0.5))
            ks.append(kh * lax.rsqrt(jnp.sum(kh * kh, axis=-1, keepdims=True) + L2_EPS))
            vs.append(y[:, 2 * nq + h * dv:2 * nq + (h + 1) * dv])
            la = a_lane + h
            ecols.append(_col(e_pos, la))
            rcols.append(_col(e_rem, la))
            elasts.append(_col(e_last, la))
            bcols.append(_col(beta, b_lane + h))
            diff = _col(gc, la) - gc_t[la:la + 1, :]
            gammas.append(jnp.where(incl, jnp.exp(jnp.where(incl, diff, 0.0)), 0.0))

    us = range(len(units))
    kbs = [ks[i] * bcols[i] for i in us]
    scs = [_mm_nt(jnp.concatenate([kbs[i], qs[i]], axis=0), ks[i]) for i in us]
    ts = _unit_lower_inverse([jnp.where(strict, scs[i][0:C] * gammas[i], 0.0) for i in us], eye, lvl0, levels)
    uws = [_mm(ts[i], jnp.concatenate([vs[i] * bcols[i], kbs[i] * ecols[i]], axis=1)) for i in us]
    ss = [state_ref[i] for i in us]
    wss = [_mm(jnp.concatenate([uws[i][:, dv:dv + dk], qs[i] * ecols[i]], axis=0), ss[i]) for i in us]
    vnews = [uws[i][:, 0:dv] - wss[i][0:C] for i in us]
    os_ = [wss[i][C:2 * C] + _mm(scs[i][C:2 * C] * gammas[i], vnews[i]) for i in us]
    for i in us:
        state_ref[i] = ss[i] * elasts[i] + _mm((ks[i] * rcols[i]).T, vnews[i])
    for i, (b, h) in enumerate(units):
        zh = qkvz_ref[b, :, nqkv + h * dv:nqkv + (h + 1) * dv]
        on = _rmsnorm(os_[i], ogain_ref[...], NORM_EPS)
        o_ref[b, :, h * dv:(h + 1) * dv] = (on * (zh * _sigmoid(zh))).astype(o_ref.dtype)


def _gdn(p, convw, alog_row, dtb_row, ogain, *, heads, dk, dv, ab_block, a_lane, b_lane):
    batch, seq, _ = p.shape
    nqkvz = 2 * heads * dk + 2 * heads * dv
    nqkv = nqkvz - heads * dv
    taps = convw.shape[0]
    const = lambda c: (0, 0)
    return pl.pallas_call(
        functools.partial(_gdn_kernel, batch=batch, heads=heads, dk=dk, dv=dv, taps=taps,
                          a_lane=a_lane, b_lane=b_lane),
        out_shape=jax.ShapeDtypeStruct((batch, seq, heads * dv), BF16),
        grid=(seq // CHUNK,),
        in_specs=[
            pl.BlockSpec((batch, CHUNK, nqkvz), lambda c: (0, c, 0)),
            pl.BlockSpec((batch, CHUNK, V7X_LANES), lambda c: (0, c, ab_block)),
            pl.BlockSpec((taps, nqkv), const),
            pl.BlockSpec((1, V7X_LANES), const),
            pl.BlockSpec((1, V7X_LANES), const),
            pl.BlockSpec((1, dv), const),
        ],
        out_specs=pl.BlockSpec((batch, CHUNK, heads * dv), lambda c: (0, c, 0)),
        scratch_shapes=[
            pltpu.VMEM((batch, CHUNK + V7X_SUBLANES, nqkv), F32),
            pltpu.VMEM((batch * heads, dk, dv), F32),
        ],
        compiler_params=_cparams(("arbitrary",)),
        name="gdn",
    )(p, p, convw, alog_row, dtb_row, ogain)


def _rwkv_kernel(r_ref, k_ref, v_ref, l_ref, mur_ref, muk_ref, muv_ref, mul_ref, w0_ref, wup_ref, a0_ref,
                 aup_ref, gup_ref, kk_ref, ka_ref, rk_ref, lnw_ref, lnb_ref, o_ref,
                 pr_ref, pk_ref, pv_ref, pl_ref, state_ref, *, batch, heads, hd, lw_cols, la_cols):
    C = CHUNK
    PAD = V7X_SUBLANES
    LW = V7X_LANES
    npairs = heads * hd // LW
    rows_all = batch * C

    @pl.when(pl.program_id(0) == 0)
    def _():
        for ref in (pr_ref, pk_ref, pv_ref, pl_ref):
            ref[...] = jnp.zeros_like(ref)
        state_ref[...] = jnp.zeros_like(state_ref)

    def shift_mix(x_ref, prev_ref, mu_ref):
        parts = []
        for b in range(batch):
            x = x_ref[b]
            row = lax.broadcasted_iota(jnp.int32, x.shape, 0)
            xprev = jnp.where(row == 0, prev_ref[b, PAD - 1:PAD, :], pltpu.roll(x, 1, 0))
            prev_ref[b] = x[C - PAD:C, :]
            parts.append(x + (xprev - x) * mu_ref[...])
        return jnp.concatenate(parts, axis=0)

    r = shift_mix(r_ref, pr_ref, mur_ref)
    k = shift_mix(k_ref, pk_ref, muk_ref)
    v = shift_mix(v_ref, pv_ref, muv_ref)
    lo = shift_mix(l_ref, pl_ref, mul_ref)

    wpre = w0_ref[...] + _mm(jnp.tanh(lo[:, 0:lw_cols]), wup_ref[...])
    lw = -jnp.exp(-_softplus(-wpre) - 0.5)
    a = _sigmoid(a0_ref[...] + _mm(lo[:, lw_cols:lw_cols + la_cols], aup_ref[...]))
    gate = _mm(_sigmoid(lo[:, lw_cols + la_cols:]), gup_ref[...])

    incl, strict, eye, lvl0, levels = _tri_masks(2 * C)
    ti = lax.broadcasted_iota(jnp.int32, (rows_all, rows_all), 0)
    tj = lax.broadcasted_iota(jnp.int32, (rows_all, rows_all), 1)
    tril = jnp.where(((ti // C) == (tj // C)) & (tj <= ti), 1.0, 0.0).astype(BF16)
    li = lax.broadcasted_iota(jnp.int32, (LW, LW), 0)
    lj = lax.broadcasted_iota(jnp.int32, (LW, LW), 1)
    headsum = jnp.where((li // hd) == (lj // hd), 1.0, 0.0).astype(BF16)
    m0 = lax.broadcasted_iota(jnp.int32, (C, LW), 1) < hd

    kkraw = k * kk_ref[...]
    k2 = k * (1.0 + (a - 1.0) * ka_ref[...])
    cum = _mm_wide_rhs(tril, lw, 3)
    clast = jnp.concatenate(
        [jnp.broadcast_to(cum[b * C + C - 1:b * C + C, :], (C, cum.shape[1])) for b in range(batch)], axis=0)
    e_pos = jnp.exp(cum)
    e_neg = jnp.exp(-cum)
    e_exc = jnp.exp(cum - lw)
    e_rem = jnp.exp(clast - cum)
    wc = jnp.exp(clast)
    rkr = r * k2 * rk_ref[...]

    def stack(x):
        return jnp.concatenate([jnp.where(m0, x, 0.0), jnp.where(m0, 0.0, x)], axis=0)

    units = [(b, p) for b in range(batch) for p in range(npairs)]
    us = range(len(units))
    win = [(slice(b * C, (b + 1) * C), slice(p * LW, (p + 1) * LW)) for b, p in units]
    sums = [_mm_wide_lhs(jnp.concatenate([kkraw[w] * kkraw[w], rkr[w]], axis=0), headsum, 1) for w in win]
    kks = [kkraw[w] * lax.rsqrt(sums[i][0:C] + L2_EPS) for i, w in enumerate(win)]
    kkas = [kks[i] * a[w] for i, w in enumerate(win)]
    a_ss = [stack(-kks[i] * e_exc[w]) for i, w in enumerate(win)]
    r_ss = [stack(r[w] * e_pos[w]) for w in win]
    b_ss = [stack(kkas[i] * e_neg[w]) for i, w in enumerate(win)]
    k_ss = [stack(k2[w] * e_neg[w]) for w in win]
    v_ss = [stack(v[w]) for w in win]
    scs = [_mm_nt(jnp.concatenate([a_ss[i], r_ss[i]], axis=0), jnp.concatenate([b_ss[i], k_ss[i]], axis=0))
           for i in us]
    lvs = [_mm(jnp.where(strict, scs[i][0:2 * C, 2 * C:4 * C], 0.0), v_ss[i]) for i in us]
    ts = _unit_lower_inverse([jnp.where(strict, -scs[i][0:2 * C, 0:2 * C], 0.0) for i in us], eye, lvl0, levels)
    tts = [_mm(ts[i], jnp.concatenate([a_ss[i], lvs[i]], axis=1)) for i in us]
    qs = [state_ref[i] for i in us]
    pqs = [_mm_nt(jnp.concatenate([tts[i][:, 0:LW], r_ss[i]], axis=0), qs[i]) for i in us]
    uvs = [jnp.concatenate([pqs[i][0:2 * C] + tts[i][:, LW:2 * LW], v_ss[i]], axis=0) for i in us]
    for i, w in enumerate(win):
        bk = jnp.concatenate([stack(kkas[i] * e_rem[w]), stack(k2[w] * e_rem[w])], axis=0)
        state_ref[i] = qs[i] * wc[w][0:1, :] + _mm(uvs[i].T, bk)
    y_ss = [pqs[i][2 * C:4 * C]
            + _mm(jnp.concatenate([jnp.where(incl, scs[i][2 * C:4 * C, 0:2 * C], 0.0),
                                   jnp.where(incl, scs[i][2 * C:4 * C, 2 * C:4 * C], 0.0)], axis=1), uvs[i])
            for i in us]
    yps = [y_s[0:C] + y_s[C:2 * C] for y_s in y_ss]

    dlts = [yp - _mm_wide_lhs(yp, headsum, 2) * (1.0 / hd) for yp in yps]
    vars_ = [_mm_wide_lhs(dlt * dlt, headsum, 1) * (1.0 / hd) for dlt in dlts]
    for i, ((b, p), w) in enumerate(zip(units, win)):
        sl = w[1]
        yn = dlts[i] * lax.rsqrt(vars_[i] + RW_GN_EPS) * lnw_ref[:, sl] + lnb_ref[:, sl]
        bonus = sums[i][C:2 * C] * v[w]
        o_ref[b, :, sl] = ((yn + bonus) * gate[w]).astype(o_ref.dtype)


def _rwkv(p, mu_r, mu_k, mu_v, mu_l, w0, wup, a0, aup, gup, k_k, k_a, r_k, ln_w, ln_b,
          *, heads, hd, r_block, lora_block, lora_width):
    batch, seq, _ = p.shape
    width = heads * hd
    const = lambda c: (0, 0)
    vec = pl.BlockSpec((1, width), const)
    lw_cols, la_cols, lg_cols = wup.shape[0], aup.shape[0], gup.shape[0]
    assert lw_cols + la_cols + lg_cols == lora_width
    return pl.pallas_call(
        functools.partial(_rwkv_kernel, batch=batch, heads=heads, hd=hd, lw_cols=lw_cols, la_cols=la_cols),
        out_shape=jax.ShapeDtypeStruct((batch, seq, width), BF16),
        grid=(seq // CHUNK,),
        in_specs=[
            pl.BlockSpec((batch, CHUNK, width), lambda c: (0, c, r_block)),
            pl.BlockSpec((batch, CHUNK, width), lambda c: (0, c, r_block + 1)),
            pl.BlockSpec((batch, CHUNK, width), lambda c: (0, c, r_block + 2)),
            pl.BlockSpec((batch, CHUNK, lora_width), lambda c: (0, c, lora_block)),
            vec, vec, vec,
            pl.BlockSpec((1, lora_width), const),
            vec,
            pl.BlockSpec((lw_cols, width), const),
            vec,
            pl.BlockSpec((la_cols, width), const),
            pl.BlockSpec((lg_cols, width), const),
            vec, vec, vec, vec, vec,
        ],
        out_specs=pl.BlockSpec((batch, CHUNK, width), lambda c: (0, c, 0)),
        scratch_shapes=[
            pltpu.VMEM((batch, V7X_SUBLANES, width), F32),
            pltpu.VMEM((batch, V7X_SUBLANES, width), F32),
            pltpu.VMEM((batch, V7X_SUBLANES, width), F32),
            pltpu.VMEM((batch, V7X_SUBLANES, lora_width), F32),
            pltpu.VMEM((batch * width // V7X_LANES, V7X_LANES, V7X_LANES), F32),
        ],
        compiler_params=_cparams(("arbitrary",)),
        name="rwkv",
    )(p, p, p, p, mu_r, mu_k, mu_v, mu_l, w0, wup, a0, aup, gup, k_k, k_a, r_k, ln_w, ln_b)


def _round_up(n, mult):
    return -(-n // mult) * mult


def _pad_cols(w, total):
    return jnp.concatenate([w, jnp.zeros((w.shape[0], total - w.shape[1]), w.dtype)], axis=1)


def _pad_rows(w, total):
    return jnp.concatenate([w, jnp.zeros((total - w.shape[0], w.shape[1]), w.dtype)], axis=0)


def _lane_row(vals, start, total):
    return jnp.zeros((1, total), F32).at[0, start:start + vals.shape[0]].set(vals)


def kernel(x, ffn1_norm, ffn1_w_gate, ffn1_w_up, ffn1_w_down, mix_norm, w_in, gdn_conv, gdn_a_log, gdn_dt_bias, gdn_out_norm, rw_mu, rw_w0, rw_w_up, rw_a0, rw_a_up, rw_g_up, rw_k_k, rw_k_a, rw_r_k, rw_ln_w, rw_ln_b, w_branch_a, w_branch_b, w_out, ffn2_norm, ffn2_w_gate, ffn2_w_up, ffn2_w_down, final_norm):
    batch, seq, d = x.shape
    depth = w_in.shape[0]
    gdn_w = w_branch_a.shape[1]
    rw_w = w_branch_b.shape[1]
    dk = gdn_w // GDN_HEADS
    dv = gdn_out_norm.shape[1]
    hd = rw_w // RW_HEADS
    lw_, la_, lg_ = rw_w_up.shape[1], rw_a_up.shape[1], rw_g_up.shape[1]
    n_qkvz = 4 * gdn_w
    n_ab = 2 * GDN_HEADS
    n_slab = 3 * rw_w + lw_ + la_ + lg_
    off_ab = n_qkvz
    off_slab = n_qkvz + n_ab
    off_lora = off_slab + 3 * rw_w
    off_ga = off_slab + n_slab

    lw_cols = _round_up(lw_ + n_ab, V7X_LANES)
    la_cols = _round_up(la_, V7X_LANES)
    lg_cols = _round_up(lg_, V7X_LANES)
    lora_width = lw_cols + la_cols + lg_cols
    n_mix = n_qkvz + 3 * rw_w + lora_width
    assert (n_qkvz + 3 * rw_w) % lora_width == 0 and n_qkvz % rw_w == 0
    ab_block = (n_qkvz + 3 * rw_w) // V7X_LANES
    a_lane = lw_
    b_lane = lw_ + GDN_HEADS

    m = batch * seq
    h = x.reshape(m, d)
    row = lambda t: t.reshape(1, -1).astype(F32)

    w_in_bf = w_in.astype(BF16)
    for l in range(depth):
        wi = w_in_bf[l]
        w_mix = jnp.concatenate(
            [wi[:, 0:off_ab], wi[:, off_slab:off_lora],
             _pad_cols(jnp.concatenate([wi[:, off_lora:off_lora + lw_], wi[:, off_ab:off_ab + n_ab]], axis=1), lw_cols),
             _pad_cols(wi[:, off_lora + lw_:off_lora + lw_ + la_], la_cols),
             _pad_cols(wi[:, off_lora + lw_ + la_:off_ga], lg_cols)], axis=1)
        wga = wi[:, off_ga:off_ga + d]
        wgb = wi[:, off_ga + d:off_ga + 2 * d]

        h = _ffn(h, row(ffn1_norm[l]), ffn1_w_gate[l].astype(BF16), ffn1_w_up[l].astype(BF16),
                 ffn1_w_down[l].astype(BF16), row(final_norm), final=False, tm=512, tf=512)

        p = _inproj(h, row(mix_norm[l]), w_mix, tm=1024, tn=n_mix // 5).reshape(batch, seq, n_mix)

        oa = _gdn(p, gdn_conv[l], _lane_row(gdn_a_log[l], a_lane, V7X_LANES),
                  _lane_row(gdn_dt_bias[l], a_lane, V7X_LANES), row(gdn_out_norm[l]),
                  heads=GDN_HEADS, dk=dk, dv=dv, ab_block=ab_block, a_lane=a_lane, b_lane=b_lane)

        mu = rw_mu[l]
        mu_l = jnp.concatenate(
            [_lane_row(mu[3 * rw_w:3 * rw_w + lw_], 0, lw_cols),
             _lane_row(mu[3 * rw_w + lw_:3 * rw_w + lw_ + la_], 0, la_cols),
             _lane_row(mu[3 * rw_w + lw_ + la_:], 0, lg_cols)], axis=1)
        ob = _rwkv(p, row(mu[0:rw_w]), row(mu[rw_w:2 * rw_w]), row(mu[2 * rw_w:3 * rw_w]), mu_l,
                   row(rw_w0[l]), _pad_rows(rw_w_up[l], lw_cols).astype(BF16),
                   row(rw_a0[l]), _pad_rows(rw_a_up[l], la_cols).astype(BF16),
                   _pad_rows(rw_g_up[l], lg_cols).astype(BF16),
                   row(rw_k_k[l]), row(rw_k_a[l]), row(rw_r_k[l]), row(rw_ln_w[l]), row(rw_ln_b[l]),
                   heads=RW_HEADS, hd=hd, r_block=n_qkvz // rw_w,
                   lora_block=(n_qkvz + 3 * rw_w) // lora_width, lora_width=lora_width)

        h = _mixout(h, row(mix_norm[l]), oa.reshape(m, gdn_w), ob.reshape(m, rw_w), wga, wgb,
                    w_branch_a[l].astype(BF16), w_branch_b[l].astype(BF16), w_out[l].astype(BF16),
                    tm=512, tj=512)

        h = _ffn(h, row(ffn2_norm[l]), ffn2_w_gate[l].astype(BF16), ffn2_w_up[l].astype(BF16),
                 ffn2_w_down[l].astype(BF16), row(final_norm), final=(l == depth - 1), tm=512, tf=512)

    return h.reshape(batch, seq, d)
```

```python
import functools

import jax
import jax.numpy as jnp
from jax import lax
from jax.experimental import pallas as pl
from jax.experimental.pallas import tpu as pltpu

F32 = jnp.float32
BF16 = jnp.bfloat16

V7X_LANES = 128
V7X_SUBLANES = 8
V7X_VMEM_BYTES = 64 * 1024 * 1024
VMEM_LIMIT_BYTES = V7X_VMEM_BYTES - 6 * 1024 * 1024

CHUNK = 64
NORM_EPS = 1e-6
L2_EPS = 1e-6
GDN_HEADS = 8
RW_HEADS = 16
RW_GN_EPS = 64e-5
SUB = 16
ROW_BLOCK = 256


def _mm(a, b):
    return jnp.dot(a.astype(BF16), b.astype(BF16), preferred_element_type=F32)


def _mm_nt(a, b):
    return lax.dot_general(a.astype(BF16), b.astype(BF16), (((1,), (1,)), ((), ())),
                           preferred_element_type=F32)


def _split_bf16(x, terms):
    parts = []
    rem = x
    for _ in range(terms):
        p = rem.astype(BF16)
        parts.append(p)
        rem = rem - p.astype(F32)
    return parts


def _mm_wide_rhs(a_bf16, x, terms):
    acc = None
    for p in reversed(_split_bf16(x, terms)):
        t = jnp.dot(a_bf16, p, preferred_element_type=F32)
        acc = t if acc is None else acc + t
    return acc


def _mm_wide_lhs(x, b_bf16, terms):
    acc = None
    for p in reversed(_split_bf16(x, terms)):
        t = jnp.dot(p, b_bf16, preferred_element_type=F32)
        acc = t if acc is None else acc + t
    return acc


def _sigmoid(x):
    return jax.nn.sigmoid(x)


def _softplus(x):
    return jnp.maximum(x, 0.0) + jnp.log1p(jnp.exp(-jnp.abs(x)))


def _rmsnorm(x, gain, eps):
    ms = jnp.mean(x * x, axis=-1, keepdims=True)
    return x * lax.rsqrt(ms + eps) * gain


def _col(x, lane):
    idx = lax.broadcasted_iota(jnp.int32, x.shape, 1)
    return jnp.sum(jnp.where(idx == lane, x, 0.0), axis=-1, keepdims=True)


def _tri_masks(n):
    i = lax.broadcasted_iota(jnp.int32, (n, n), 0)
    j = lax.broadcasted_iota(jnp.int32, (n, n), 1)
    same = (i // CHUNK) == (j // CHUNK)
    incl = same & (j <= i)
    strict = same & (j < i)
    eye = jnp.where(i == j, 1.0, 0.0).astype(F32)
    lvl0 = (i // SUB) == (j // SUB)
    levels = []
    size = SUB
    while size < CHUNK:
        levels.append(((i // (2 * size)) == (j // (2 * size))) & ((i // size) != (j // size)))
        size *= 2
    return incl, strict, eye, lvl0, levels


def _unit_lower_inverse(mats, eye, lvl0, levels):
    pws = [jnp.where(lvl0, a, 0.0) for a in mats]
    ts = [eye - d for d in pws]
    span = 2
    while span < SUB:
        pws = [_mm(pw, pw) for pw in pws]
        ts = [t + _mm(t, pw) for t, pw in zip(ts, pws)]
        span *= 2
    for m in levels:
        tes = [_mm(t, jnp.where(m, a, 0.0)) for t, a in zip(ts, mats)]
        ts = [t - _mm(te, t) for t, te in zip(ts, tes)]
    return ts


def _for_row_blocks(rows, body):
    step = min(ROW_BLOCK, rows)

    def step_fn(i, carry):
        body(pl.ds(pl.multiple_of(i * step, step), step))
        return carry

    lax.fori_loop(0, rows // step, step_fn, 0)


def _cparams(sem):
    return pltpu.CompilerParams(dimension_semantics=sem, vmem_limit_bytes=VMEM_LIMIT_BYTES)


def _ffn_kernel(x_ref, gain_ref, wg_ref, wu_ref, wd_ref, fgain_ref, o_ref, xn_ref, *, nj, final):
    j = pl.program_id(1)

    @pl.when(j == 0)
    def _():
        def body(rows):
            x = x_ref[rows, :]
            xn_ref[rows, :] = _rmsnorm(x, gain_ref[...], NORM_EPS).astype(BF16)
            o_ref[rows, :] = x

        _for_row_blocks(x_ref.shape[0], body)

    xn = xn_ref[...]
    g = jnp.dot(xn, wg_ref[...], preferred_element_type=F32)
    u = jnp.dot(xn, wu_ref[...], preferred_element_type=F32)
    act = (0.5 * (g * _sigmoid(g) * u)).astype(BF16)
    o_ref[...] += jnp.dot(act, wd_ref[...], preferred_element_type=F32)

    if final:
        @pl.when(j == nj - 1)
        def _():
            def body(rows):
                o_ref[rows, :] = _rmsnorm(o_ref[rows, :], fgain_ref[...], NORM_EPS)

            _for_row_blocks(o_ref.shape[0], body)


def _ffn(h, gain, wg, wu, wd, fgain, *, final, tm, tf):
    m, d = h.shape
    f = wg.shape[1]
    nj = f // tf
    return pl.pallas_call(
        functools.partial(_ffn_kernel, nj=nj, final=final),
        out_shape=jax.ShapeDtypeStruct((m, d), F32),
        grid=(m // tm, nj),
        in_specs=[
            pl.BlockSpec((tm, d), lambda i, j: (i, 0)),
            pl.BlockSpec((1, d), lambda i, j: (0, 0)),
            pl.BlockSpec((d, tf), lambda i, j: (0, j)),
            pl.BlockSpec((d, tf), lambda i, j: (0, j)),
            pl.BlockSpec((tf, d), lambda i, j: (j, 0)),
            pl.BlockSpec((1, d), lambda i, j: (0, 0)),
        ],
        out_specs=pl.BlockSpec((tm, d), lambda i, j: (i, 0)),
        scratch_shapes=[pltpu.VMEM((tm, d), BF16)],
        compiler_params=_cparams(("parallel", "arbitrary")),
        name="ffn",
    )(h, gain, wg, wu, wd, fgain)


def _inproj_kernel(x_ref, gain_ref, w_ref, o_ref, xn_ref):
    @pl.when(pl.program_id(1) == 0)
    def _():
        def body(rows):
            xn_ref[rows, :] = _rmsnorm(x_ref[rows, :], gain_ref[...], NORM_EPS).astype(BF16)

        _for_row_blocks(x_ref.shape[0], body)

    o_ref[...] = jnp.dot(xn_ref[...], w_ref[...], preferred_element_type=F32)


def _inproj(h, gain, w, *, tm, tn):
    m, d = h.shape
    n = w.shape[1]
    return pl.pallas_call(
        _inproj_kernel,
        out_shape=jax.ShapeDtypeStruct((m, n), F32),
        grid=(m // tm, n // tn),
        in_specs=[
            pl.BlockSpec((tm, d), lambda i, j: (i, 0)),
            pl.BlockSpec((1, d), lambda i, j: (0, 0)),
            pl.BlockSpec((d, tn), lambda i, j: (0, j)),
        ],
        out_specs=pl.BlockSpec((tm, tn), lambda i, j: (i, j)),
        scratch_shapes=[pltpu.VMEM((tm, d), BF16)],
        compiler_params=_cparams(("parallel", "arbitrary")),
        name="inproj",
    )(h, gain, w)


def _mixout_kernel(x_ref, gain_ref, oa_ref, ob_ref, wga_ref, wgb_ref, wba_ref, wbb_ref, wo_ref,
                   o_ref, xn_ref):
    @pl.when(pl.program_id(1) == 0)
    def _():
        def body(rows):
            x = x_ref[rows, :]
            xn_ref[rows, :] = _rmsnorm(x, gain_ref[...], NORM_EPS).astype(BF16)
            o_ref[rows, :] = x

        _for_row_blocks(x_ref.shape[0], body)

    xn = xn_ref[...]
    ga = jnp.dot(xn, wga_ref[...], preferred_element_type=F32)
    gb = jnp.dot(xn, wgb_ref[...], preferred_element_type=F32)
    ya = jnp.dot(oa_ref[...], wba_ref[...], preferred_element_type=F32)
    yb = jnp.dot(ob_ref[...], wbb_ref[...], preferred_element_type=F32)
    y = (_sigmoid(ga) * ya + _sigmoid(gb) * yb).astype(BF16)
    o_ref[...] += jnp.dot(y, wo_ref[...], preferred_element_type=F32)


def _mixout(h, gain, oa, ob, w_gates, wba, wbb, wo, *, tm, tj):
    m, d = h.shape
    na = oa.shape[1]
    nb = ob.shape[1]
    dy = w_gates.shape[1] // 2
    nj = dy // tj
    return pl.pallas_call(
        _mixout_kernel,
        out_shape=jax.ShapeDtypeStruct((m, d), F32),
        grid=(m // tm, nj),
        in_specs=[
            pl.BlockSpec((tm, d), lambda i, j: (i, 0)),
            pl.BlockSpec((1, d), lambda i, j: (0, 0)),
            pl.BlockSpec((tm, na), lambda i, j: (i, 0)),
            pl.BlockSpec((tm, nb), lambda i, j: (i, 0)),
            pl.BlockSpec((d, tj), lambda i, j: (0, j)),
            pl.BlockSpec((d, tj), lambda i, j: (0, j + nj)),
            pl.BlockSpec((na, tj), lambda i, j: (0, j)),
            pl.BlockSpec((nb, tj), lambda i, j: (0, j)),
            pl.BlockSpec((tj, d), lambda i, j: (j, 0)),
        ],
        out_specs=pl.BlockSpec((tm, d), lambda i, j: (i, 0)),
        scratch_shapes=[pltpu.VMEM((tm, d), BF16)],
        compiler_params=_cparams(("parallel", "arbitrary")),
        name="mixout",
    )(h, gain, oa, ob, w_gates, w_gates, wba, wbb, wo)


def _gdn_kernel(qkvz_ref, ab_ref, convw_ref, alog_ref, dtb_ref, ogain_ref, o_ref, xpad_ref, state_ref,
                *, batch, heads, dk, dv, taps, a_lane, b_lane):
    C = CHUNK
    PAD = V7X_SUBLANES
    nq = heads * dk
    nqkv = 2 * nq + heads * dv

    @pl.when(pl.program_id(0) == 0)
    def _():
        for b in range(batch):
            xpad_ref[b, 0:PAD, :] = jnp.zeros((PAD, nqkv), F32)
        state_ref[...] = jnp.zeros_like(state_ref)

    incl, strict, eye, lvl0, levels = _tri_masks(C)
    tril = jnp.where(incl, 1.0, 0.0).astype(BF16)

    units = []
    qs, ks, vs, ecols, rcols, elasts, bcols, gammas = [], [], [], [], [], [], [], []
    for b in range(batch):
        x = qkvz_ref[b, :, 0:nqkv]
        xpad_ref[b, PAD:PAD + C, :] = x
        acc = None
        for j in range(taps):
            term = xpad_ref[b, pl.ds(PAD - (taps - 1) + j, C), :] * convw_ref[j:j + 1, :]
            acc = term if acc is None else acc + term
        xpad_ref[b, 0:PAD, :] = x[C - PAD:C, :]
        y = acc * _sigmoid(acc)

        ab = ab_ref[b]
        g = -jnp.exp(alog_ref[...]) * _softplus(ab + dtb_ref[...])
        beta = _sigmoid(ab)
        gc = _mm_wide_rhs(tril, g, 3)
        glast = gc[C - 1:C, :]
        e_pos = jnp.exp(gc)
        e_rem = jnp.exp(glast - gc)
        e_last = jnp.exp(glast)
        gc_t = gc.T
        for h in range(heads):
            units.append((b, h))
            qh = y[:, h * dk:(h + 1) * dk]
            kh = y[:, nq + h * dk:nq + (h + 1) * dk]
            qs.append(qh * lax.rsqrt(jnp.sum(qh * qh, axis=-1, keepdims=True) + L2_EPS) * (dk ** -0.5))
            ks.append(kh * lax.rsqrt(jnp.sum(kh * kh, axis=-1, keepdims=True) + L2_EPS))
            vs.append(y[:, 2 * nq + h * dv:2 * nq + (h + 1) * dv])
            la = a_lane + h
            ecols.append(_col(e_pos, la))
            rcols.append(_col(e_rem, la))
            elasts.append(_col(e_last, la))
            bcols.append(_col(beta, b_lane + h))
            diff = _col(gc, la) - gc_t[la:la + 1, :]
            gammas.append(jnp.where(incl, jnp.exp(jnp.where(incl, diff, 0.0)), 0.0))

    us = range(len(units))
    kbs = [ks[i] * bcols[i] for i in us]
    scs = [_mm_nt(jnp.concatenate([kbs[i], qs[i]], axis=0), ks[i]) for i in us]
    ts = _unit_lower_inverse([jnp.where(strict, scs[i][0:C] * gammas[i], 0.0) for i in us], eye, lvl0, levels)
    uws = [_mm(ts[i], jnp.concatenate([vs[i] * bcols[i], kbs[i] * ecols[i]], axis=1)) for i in us]
    ss = [state_ref[i] for i in us]
    wss = [_mm(jnp.concatenate([uws[i][:, dv:dv + dk], qs[i] * ecols[i]], axis=0), ss[i]) for i in us]
    vnews = [uws[i][:, 0:dv] - wss[i][0:C] for i in us]
    os_ = [wss[i][C:2 * C] + _mm(scs[i][C:2 * C] * gammas[i], vnews[i]) for i in us]
    for i in us:
        state_ref[i] = ss[i] * elasts[i] + _mm((ks[i] * rcols[i]).T, vnews[i])
    for i, (b, h) in enumerate(units):
        zh = qkvz_ref[b, :, nqkv + h * dv:nqkv + (h + 1) * dv]
        on = _rmsnorm(os_[i], ogain_ref[...], NORM_EPS)
        o_ref[b, :, h * dv:(h + 1) * dv] = (on * (zh * _sigmoid(zh))).astype(o_ref.dtype)


def _gdn(p, convw, alog_row, dtb_row, ogain, *, heads, dk, dv, ab_block, a_lane, b_lane):
    batch, seq, _ = p.shape
    nqkvz = 2 * heads * dk + 2 * heads * dv
    nqkv = nqkvz - heads * dv
    taps = convw.shape[0]
    const = lambda c: (0, 0)
    return pl.pallas_call(
        functools.partial(_gdn_kernel, batch=batch, heads=heads, dk=dk, dv=dv, taps=taps,
                          a_lane=a_lane, b_lane=b_lane),
        out_shape=jax.ShapeDtypeStruct((batch, seq, heads * dv), BF16),
        grid=(seq // CHUNK,),
        in_specs=[
            pl.BlockSpec((batch, CHUNK, nqkvz), lambda c: (0, c, 0)),
            pl.BlockSpec((batch, CHUNK, V7X_LANES), lambda c: (0, c, ab_block)),
            pl.BlockSpec((taps, nqkv), const),
            pl.BlockSpec((1, V7X_LANES), const),
            pl.BlockSpec((1, V7X_LANES), const),
            pl.BlockSpec((1, dv), const),
        ],
        out_specs=pl.BlockSpec((batch, CHUNK, heads * dv), lambda c: (0, c, 0)),
        scratch_shapes=[
            pltpu.VMEM((batch, CHUNK + V7X_SUBLANES, nqkv), F32),
            pltpu.VMEM((batch * heads, dk, dv), F32),
        ],
        compiler_params=_cparams(("arbitrary",)),
        name="gdn",
    )(p, p, convw, alog_row, dtb_row, ogain)


def _rwkv_kernel(r_ref, k_ref, v_ref, l_ref, mur_ref, muk_ref, muv_ref, mul_ref, w0_ref, wup_ref, a0_ref,
                 aup_ref, gup_ref, kk_ref, ka_ref, rk_ref, lnw_ref, lnb_ref, o_ref,
                 pr_ref, pk_ref, pv_ref, pl_ref, state_ref, *, batch, heads, hd, lw_cols, la_cols):
    C = CHUNK
    PAD = V7X_SUBLANES
    LW = V7X_LANES
    npairs = heads * hd // LW
    rows_all = batch * C
    assert LW == 2 * hd == 2 * C

    @pl.when(pl.program_id(0) == 0)
    def _():
        for ref in (pr_ref, pk_ref, pv_ref, pl_ref):
            ref[...] = jnp.zeros_like(ref)
        state_ref[...] = jnp.zeros_like(state_ref)

    def shift_mix(x_ref, prev_ref, mu_ref):
        parts = []
        for b in range(batch):
            x = x_ref[b]
            row = lax.broadcasted_iota(jnp.int32, x.shape, 0)
            xprev = jnp.where(row == 0, prev_ref[b, PAD - 1:PAD, :], pltpu.roll(x, 1, 0))
            prev_ref[b] = x[C - PAD:C, :]
            parts.append(x + (xprev - x) * mu_ref[...])
        return jnp.concatenate(parts, axis=0)

    r = shift_mix(r_ref, pr_ref, mur_ref)
    k = shift_mix(k_ref, pk_ref, muk_ref)
    v = shift_mix(v_ref, pv_ref, muv_ref)
    lo = shift_mix(l_ref, pl_ref, mul_ref)

    wpre = w0_ref[...] + _mm(jnp.tanh(lo[:, 0:lw_cols]), wup_ref[...])
    lw = -jnp.exp(-_softplus(-wpre) - 0.5)
    a = _sigmoid(a0_ref[...] + _mm(lo[:, lw_cols:lw_cols + la_cols], aup_ref[...]))
    gate = _mm(_sigmoid(lo[:, lw_cols + la_cols:]), gup_ref[...])

    incl, strict, eye, lvl0, levels = _tri_masks(2 * C)
    ti = lax.broadcasted_iota(jnp.int32, (rows_all, rows_all), 0)
    tj = lax.broadcasted_iota(jnp.int32, (rows_all, rows_all), 1)
    tril = jnp.where(((ti // C) == (tj // C)) & (tj <= ti), 1.0, 0.0).astype(BF16)
    li = lax.broadcasted_iota(jnp.int32, (LW, LW), 0)
    lj = lax.broadcasted_iota(jnp.int32, (LW, LW), 1)
    headsum = jnp.where((li // hd) == (lj // hd), 1.0, 0.0).astype(BF16)
    m0 = lax.broadcasted_iota(jnp.int32, (C, LW), 1) < hd

    kkraw = k * kk_ref[...]
    k2 = k * (1.0 + (a - 1.0) * ka_ref[...])
    cum = _mm_wide_rhs(tril, lw, 3)
    clast = jnp.concatenate(
        [jnp.broadcast_to(cum[b * C + C - 1:b * C + C, :], (C, cum.shape[1])) for b in range(batch)], axis=0)
    e_pos = jnp.exp(cum)
    e_neg = jnp.exp(-cum)
    e_exc = jnp.exp(cum - lw)
    e_rem = jnp.exp(clast - cum)
    wc = jnp.exp(clast)
    rkr = r * k2 * rk_ref[...]

    def stack(x):
        return jnp.concatenate([jnp.where(m0, x, 0.0), jnp.where(m0, 0.0, x)], axis=0)

    units = [(b, p) for b in range(batch) for p in range(npairs)]
    us = range(len(units))
    win = [(slice(b * C, (b + 1) * C), slice(p * LW, (p + 1) * LW)) for b, p in units]
    sums = [_mm_wide_lhs(jnp.concatenate([kkraw[w] * kkraw[w], rkr[w]], axis=0), headsum, 1) for w in win]
    kks = [kkraw[w] * lax.rsqrt(sums[i][0:C] + L2_EPS) for i, w in enumerate(win)]
    kkas = [kks[i] * a[w] for i, w in enumerate(win)]
    a_ss = [stack(-kks[i] * e_exc[w]) for i, w in enumerate(win)]
    r_ss = [stack(r[w] * e_pos[w]) for w in win]
    b_ss = [stack(kkas[i] * e_neg[w]) for i, w in enumerate(win)]
    k_ss = [stack(k2[w] * e_neg[w]) for w in win]
    v_ss = [stack(v[w]) for w in win]
    scs = [_mm_nt(jnp.concatenate([a_ss[i], r_ss[i]], axis=0), jnp.concatenate([b_ss[i], k_ss[i]], axis=0))
           for i in us]
    lvs = [_mm(jnp.where(strict, scs[i][0:2 * C, 2 * C:4 * C], 0.0), v_ss[i]) for i in us]
    ts = _unit_lower_inverse([jnp.where(strict, -scs[i][0:2 * C, 0:2 * C], 0.0) for i in us], eye, lvl0, levels)
    tts = [_mm(ts[i], jnp.concatenate([a_ss[i], lvs[i]], axis=1)) for i in us]
    qs = [state_ref[i] for i in us]
    pqs = [_mm_nt(jnp.concatenate([tts[i][:, 0:LW], r_ss[i]], axis=0), qs[i]) for i in us]
    uvs = [jnp.concatenate([pqs[i][0:2 * C] + tts[i][:, LW:2 * LW], v_ss[i]], axis=0) for i in us]
    for i, w in enumerate(win):
        bk = jnp.concatenate([stack(kkas[i] * e_rem[w]), stack(k2[w] * e_rem[w])], axis=0)
        state_ref[i] = qs[i] * wc[w][0:1, :] + _mm(uvs[i].T, bk)
    y_ss = [pqs[i][2 * C:4 * C]
            + _mm(jnp.concatenate([jnp.where(incl, scs[i][2 * C:4 * C, 0:2 * C], 0.0),
                                   jnp.where(incl, scs[i][2 * C:4 * C, 2 * C:4 * C], 0.0)], axis=1), uvs[i])
            for i in us]
    yps = [y_s[0:C] + y_s[C:2 * C] for y_s in y_ss]

    dlts = [yp - _mm_wide_lhs(yp, headsum, 2) * (1.0 / hd) for yp in yps]
    vars_ = [_mm_wide_lhs(dlt * dlt, headsum, 1) * (1.0 / hd) for dlt in dlts]
    for i, ((b, p), w) in enumerate(zip(units, win)):
        sl = w[1]
        yn = dlts[i] * lax.rsqrt(vars_[i] + RW_GN_EPS) * lnw_ref[:, sl] + lnb_ref[:, sl]
        bonus = sums[i][C:2 * C] * v[w]
        o_ref[b, :, sl] = ((yn + bonus) * gate[w]).astype(o_ref.dtype)


def _rwkv(p, mu_r, mu_k, mu_v, mu_l, w0, wup, a0, aup, gup, k_k, k_a, r_k, ln_w, ln_b,
          *, heads, hd, r_block, lora_block, lora_width):
    batch, seq, _ = p.shape
    width = heads * hd
    const = lambda c: (0, 0)
    vec = pl.BlockSpec((1, width), const)
    lw_cols, la_cols, lg_cols = wup.shape[0], aup.shape[0], gup.shape[0]
    assert lw_cols + la_cols + lg_cols == lora_width
    return pl.pallas_call(
        functools.partial(_rwkv_kernel, batch=batch, heads=heads, hd=hd, lw_cols=lw_cols, la_cols=la_cols),
        out_shape=jax.ShapeDtypeStruct((batch, seq, width), BF16),
        grid=(seq // CHUNK,),
        in_specs=[
            pl.BlockSpec((batch, CHUNK, width), lambda c: (0, c, r_block)),
            pl.BlockSpec((batch, CHUNK, width), lambda c: (0, c, r_block + 1)),
            pl.BlockSpec((batch, CHUNK, width), lambda c: (0, c, r_block + 2)),
            pl.BlockSpec((batch, CHUNK, lora_width), lambda c: (0, c, lora_block)),
            vec, vec, vec,
            pl.BlockSpec((1, lora_width), const),
            vec,
            pl.BlockSpec((lw_cols, width), const),
            vec,
            pl.BlockSpec((la_cols, width), const),
            pl.BlockSpec((lg_cols, width), const),
            vec, vec, vec, vec, vec,
        ],
        out_specs=pl.BlockSpec((batch, CHUNK, width), lambda c: (0, c, 0)),
        scratch_shapes=[
            pltpu.VMEM((batch, V7X_SUBLANES, width), F32),
            pltpu.VMEM((batch, V7X_SUBLANES, width), F32),
            pltpu.VMEM((batch, V7X_SUBLANES, width), F32),
            pltpu.VMEM((batch, V7X_SUBLANES, lora_width), F32),
            pltpu.VMEM((batch * width // V7X_LANES, V7X_LANES, V7X_LANES), F32),
        ],
        compiler_params=_cparams(("arbitrary",)),
        name="rwkv",
    )(p, p, p, p, mu_r, mu_k, mu_v, mu_l, w0, wup, a0, aup, gup, k_k, k_a, r_k, ln_w, ln_b)


def _round_up(n, mult):
    return -(-n // mult) * mult


def _pad_cols(w, total):
    return jnp.concatenate([w, jnp.zeros((w.shape[0], total - w.shape[1]), w.dtype)], axis=1)


def _pad_rows(w, total):
    return jnp.concatenate([w, jnp.zeros((total - w.shape[0], w.shape[1]), w.dtype)], axis=0)


def _lane_row(vals, start, total):
    return jnp.zeros((1, total), F32).at[0, start:start + vals.shape[0]].set(vals)


def kernel(x, ffn1_norm, ffn1_w_gate, ffn1_w_up, ffn1_w_down, mix_norm, w_in, gdn_conv, gdn_a_log, gdn_dt_bias, gdn_out_norm, rw_mu, rw_w0, rw_w_up, rw_a0, rw_a_up, rw_g_up, rw_k_k, rw_k_a, rw_r_k, rw_ln_w, rw_ln_b, w_branch_a, w_branch_b, w_out, ffn2_norm, ffn2_w_gate, ffn2_w_up, ffn2_w_down, final_norm):
    batch, seq, d = x.shape
    depth = w_in.shape[0]
    gdn_w = w_branch_a.shape[1]
    rw_w = w_branch_b.shape[1]
    dk = gdn_w // GDN_HEADS
    dv = gdn_out_norm.shape[1]
    hd = rw_w // RW_HEADS
    lw_, la_, lg_ = rw_w_up.shape[1], rw_a_up.shape[1], rw_g_up.shape[1]
    n_qkvz = 4 * gdn_w
    n_ab = 2 * GDN_HEADS
    n_slab = 3 * rw_w + lw_ + la_ + lg_
    off_ab = n_qkvz
    off_slab = n_qkvz + n_ab
    off_lora = off_slab + 3 * rw_w
    off_ga = off_slab + n_slab

    lw_cols = _round_up(lw_ + n_ab, V7X_LANES)
    la_cols = _round_up(la_, V7X_LANES)
    lg_cols = _round_up(lg_, V7X_LANES)
    lora_width = lw_cols + la_cols + lg_cols
    n_mix = n_qkvz + 3 * rw_w + lora_width
    assert (n_qkvz + 3 * rw_w) % lora_width == 0 and n_qkvz % rw_w == 0
    ab_block = (n_qkvz + 3 * rw_w) // V7X_LANES
    a_lane = lw_
    b_lane = lw_ + GDN_HEADS

    m = batch * seq
    h = x.reshape(m, d)
    row = lambda t: t.reshape(1, -1).astype(F32)

    zeros = lambda n: jnp.zeros((depth, d, n), w_in.dtype)
    w_mix_all = jnp.concatenate(
        [w_in[:, :, 0:off_ab], w_in[:, :, off_slab:off_lora],
         w_in[:, :, off_lora:off_lora + lw_], w_in[:, :, off_ab:off_ab + n_ab], zeros(lw_cols - lw_ - n_ab),
         w_in[:, :, off_lora + lw_:off_lora + lw_ + la_], zeros(la_cols - la_),
         w_in[:, :, off_lora + lw_ + la_:off_ga], zeros(lg_cols - lg_)], axis=2).astype(BF16)
    w_gates_all = w_in[:, :, off_ga:off_ga + 2 * d].astype(BF16)
    for l in range(depth):
        w_mix = w_mix_all[l]
        w_gates = w_gates_all[l]

        h = _ffn(h, row(ffn1_norm[l]), ffn1_w_gate[l].astype(BF16), ffn1_w_up[l].astype(BF16),
                 ffn1_w_down[l].astype(BF16), row(final_norm), final=False, tm=1024, tf=512)

        p = _inproj(h, row(mix_norm[l]), w_mix, tm=1024, tn=n_mix // 5).reshape(batch, seq, n_mix)

        oa = _gdn(p, gdn_conv[l], _lane_row(gdn_a_log[l], a_lane, V7X_LANES),
                  _lane_row(gdn_dt_bias[l], a_lane, V7X_LANES), row(gdn_out_norm[l]),
                  heads=GDN_HEADS, dk=dk, dv=dv, ab_block=ab_block, a_lane=a_lane, b_lane=b_lane)

        mu = rw_mu[l]
        mu_l = jnp.concatenate(
            [_lane_row(mu[3 * rw_w:3 * rw_w + lw_], 0, lw_cols),
             _lane_row(mu[3 * rw_w + lw_:3 * rw_w + lw_ + la_], 0, la_cols),
             _lane_row(mu[3 * rw_w + lw_ + la_:], 0, lg_cols)], axis=1)
        ob = _rwkv(p, row(mu[0:rw_w]), row(mu[rw_w:2 * rw_w]), row(mu[2 * rw_w:3 * rw_w]), mu_l,
                   row(rw_w0[l]), _pad_rows(rw_w_up[l], lw_cols).astype(BF16),
                   row(rw_a0[l]), _pad_rows(rw_a_up[l], la_cols).astype(BF16),
                   _pad_rows(rw_g_up[l], lg_cols).astype(BF16),
                   row(rw_k_k[l]), row(rw_k_a[l]), row(rw_r_k[l]), row(rw_ln_w[l]), row(rw_ln_b[l]),
                   heads=RW_HEADS, hd=hd, r_block=n_qkvz // rw_w,
                   lora_block=(n_qkvz + 3 * rw_w) // lora_width, lora_width=lora_width)

        h = _mixout(h, row(mix_norm[l]), oa.reshape(m, gdn_w), ob.reshape(m, rw_w), w_gates,
                    w_branch_a[l].astype(BF16), w_branch_b[l].astype(BF16), w_out[l].astype(BF16),
                    tm=512, tj=512)

        h = _ffn(h, row(ffn2_norm[l]), ffn2_w_gate[l].astype(BF16), ffn2_w_up[l].astype(BF16),
                 ffn2_w_down[l].astype(BF16), row(final_norm), final=(l == depth - 1), tm=1024, tf=512)

    return h.reshape(batch, seq, d)
```

```python
import functools

import jax
import jax.numpy as jnp
from jax import lax
from jax.experimental import pallas as pl
from jax.experimental.pallas import tpu as pltpu

F32 = jnp.float32
BF16 = jnp.bfloat16

V7X_LANES = 128
V7X_SUBLANES = 8
V7X_VMEM_BYTES = 64 * 1024 * 1024
VMEM_LIMIT_BYTES = V7X_VMEM_BYTES - 6 * 1024 * 1024

CHUNK = 64
NORM_EPS = 1e-6
L2_EPS = 1e-6
GDN_HEADS = 8
RW_HEADS = 16
RW_GN_EPS = 64e-5
SUB = 16
ROW_BLOCK = 256


def _mm(a, b):
    return jnp.dot(a.astype(BF16), b.astype(BF16), preferred_element_type=F32)


def _mm_nt(a, b):
    return lax.dot_general(a.astype(BF16), b.astype(BF16), (((1,), (1,)), ((), ())),
                           preferred_element_type=F32)


def _split_bf16(x, terms):
    parts = []
    rem = x
    for _ in range(terms):
        p = rem.astype(BF16)
        parts.append(p)
        rem = rem - p.astype(F32)
    return parts


def _mm_wide_rhs(a_bf16, x, terms):
    acc = None
    for p in reversed(_split_bf16(x, terms)):
        t = jnp.dot(a_bf16, p, preferred_element_type=F32)
        acc = t if acc is None else acc + t
    return acc


def _mm_wide_lhs(x, b_bf16, terms):
    acc = None
    for p in reversed(_split_bf16(x, terms)):
        t = jnp.dot(p, b_bf16, preferred_element_type=F32)
        acc = t if acc is None else acc + t
    return acc


def _sigmoid(x):
    return jax.nn.sigmoid(x)


def _softplus(x):
    return jnp.maximum(x, 0.0) + jnp.log1p(jnp.exp(-jnp.abs(x)))


def _rmsnorm(x, gain, eps):
    ms = jnp.mean(x * x, axis=-1, keepdims=True)
    return x * lax.rsqrt(ms + eps) * gain


def _col(x, lane):
    idx = lax.broadcasted_iota(jnp.int32, x.shape, 1)
    return jnp.sum(jnp.where(idx == lane, x, 0.0), axis=-1, keepdims=True)


def _tri_masks(n):
    i = lax.broadcasted_iota(jnp.int32, (n, n), 0)
    j = lax.broadcasted_iota(jnp.int32, (n, n), 1)
    same = (i // CHUNK) == (j // CHUNK)
    incl = same & (j <= i)
    strict = same & (j < i)
    eye = jnp.where(i == j, 1.0, 0.0).astype(F32)
    lvl0 = (i // SUB) == (j // SUB)
    levels = []
    size = SUB
    while size < CHUNK:
        levels.append(((i // (2 * size)) == (j // (2 * size))) & ((i // size) != (j // size)))
        size *= 2
    return incl, strict, eye, lvl0, levels


def _unit_lower_inverse(mats, eye, lvl0, levels):
    pws = [jnp.where(lvl0, a, 0.0) for a in mats]
    ts = [eye - d for d in pws]
    span = 2
    while span < SUB:
        pws = [_mm(pw, pw) for pw in pws]
        ts = [t + _mm(t, pw) for t, pw in zip(ts, pws)]
        span *= 2
    for m in levels:
        tes = [_mm(t, jnp.where(m, a, 0.0)) for t, a in zip(ts, mats)]
        ts = [t - _mm(te, t) for t, te in zip(ts, tes)]
    return ts


def _for_row_blocks(rows, body):
    step = min(ROW_BLOCK, rows)

    def step_fn(i, carry):
        body(pl.ds(pl.multiple_of(i * step, step), step))
        return carry

    lax.fori_loop(0, rows // step, step_fn, 0)


def _cparams(sem):
    return pltpu.CompilerParams(dimension_semantics=sem, vmem_limit_bytes=VMEM_LIMIT_BYTES)


def _ffn_kernel(x_ref, gain_ref, wg_ref, wu_ref, wd_ref, fgain_ref, o_ref, xn_ref, *, nj, final):
    j = pl.program_id(1)

    @pl.when(j == 0)
    def _():
        def body(rows):
            x = x_ref[rows, :]
            xn_ref[rows, :] = _rmsnorm(x, gain_ref[...], NORM_EPS).astype(BF16)
            o_ref[rows, :] = x

        _for_row_blocks(x_ref.shape[0], body)

    xn = xn_ref[...]
    g = jnp.dot(xn, wg_ref[...], preferred_element_type=F32)
    u = jnp.dot(xn, wu_ref[...], preferred_element_type=F32)
    act = (0.5 * (g * _sigmoid(g) * u)).astype(BF16)
    o_ref[...] += jnp.dot(act, wd_ref[...], preferred_element_type=F32)

    if final:
        @pl.when(j == nj - 1)
        def _():
            def body(rows):
                o_ref[rows, :] = _rmsnorm(o_ref[rows, :], fgain_ref[...], NORM_EPS)

            _for_row_blocks(o_ref.shape[0], body)


def _ffn(h, gain, wg, wu, wd, fgain, *, layer, final, tm, tf):
    m, d = h.shape
    f = wg.shape[2]
    nj = f // tf
    return pl.pallas_call(
        functools.partial(_ffn_kernel, nj=nj, final=final),
        out_shape=jax.ShapeDtypeStruct((m, d), F32),
        grid=(m // tm, nj),
        in_specs=[
            pl.BlockSpec((tm, d), lambda i, j: (i, 0)),
            pl.BlockSpec((1, d), lambda i, j: (0, 0)),
            pl.BlockSpec((None, d, tf), lambda i, j: (layer, 0, j)),
            pl.BlockSpec((None, d, tf), lambda i, j: (layer, 0, j)),
            pl.BlockSpec((None, tf, d), lambda i, j: (layer, j, 0)),
            pl.BlockSpec((1, d), lambda i, j: (0, 0)),
        ],
        out_specs=pl.BlockSpec((tm, d), lambda i, j: (i, 0)),
        scratch_shapes=[pltpu.VMEM((tm, d), BF16)],
        compiler_params=_cparams(("parallel", "arbitrary")),
        name="ffn",
    )(h, gain, wg, wu, wd, fgain)


def _inproj_kernel(x_ref, gain_ref, w_ref, o_ref, xn_ref):
    @pl.when(pl.program_id(1) == 0)
    def _():
        def body(rows):
            xn_ref[rows, :] = _rmsnorm(x_ref[rows, :], gain_ref[...], NORM_EPS).astype(BF16)

        _for_row_blocks(x_ref.shape[0], body)

    o_ref[...] = jnp.dot(xn_ref[...], w_ref[...], preferred_element_type=F32)


def _inproj(h, gain, w, *, layer, tm, tn):
    m, d = h.shape
    n = w.shape[2]
    return pl.pallas_call(
        _inproj_kernel,
        out_shape=jax.ShapeDtypeStruct((m, n), F32),
        grid=(m // tm, n // tn),
        in_specs=[
            pl.BlockSpec((tm, d), lambda i, j: (i, 0)),
            pl.BlockSpec((1, d), lambda i, j: (0, 0)),
            pl.BlockSpec((None, d, tn), lambda i, j: (layer, 0, j)),
        ],
        out_specs=pl.BlockSpec((tm, tn), lambda i, j: (i, j)),
        scratch_shapes=[pltpu.VMEM((tm, d), BF16)],
        compiler_params=_cparams(("parallel", "arbitrary")),
        name="inproj",
    )(h, gain, w)


def _mixout_kernel(x_ref, gain_ref, oa_ref, ob_ref, wga_ref, wgb_ref, wba_ref, wbb_ref, wo_ref,
                   o_ref, xn_ref):
    @pl.when(pl.program_id(1) == 0)
    def _():
        def body(rows):
            x = x_ref[rows, :]
            xn_ref[rows, :] = _rmsnorm(x, gain_ref[...], NORM_EPS).astype(BF16)
            o_ref[rows, :] = x

        _for_row_blocks(x_ref.shape[0], body)

    xn = xn_ref[...]
    ga = jnp.dot(xn, wga_ref[...], preferred_element_type=F32)
    gb = jnp.dot(xn, wgb_ref[...], preferred_element_type=F32)
    ya = jnp.dot(oa_ref[...], wba_ref[...], preferred_element_type=F32)
    yb = jnp.dot(ob_ref[...], wbb_ref[...], preferred_element_type=F32)
    y = (_sigmoid(ga) * ya + _sigmoid(gb) * yb).astype(BF16)
    o_ref[...] += jnp.dot(y, wo_ref[...], preferred_element_type=F32)


def _mixout(h, gain, oa, ob, w_gates, wba, wbb, wo, *, layer, tm, tj):
    m, d = h.shape
    na = oa.shape[1]
    nb = ob.shape[1]
    dy = w_gates.shape[2] // 2
    nj = dy // tj
    return pl.pallas_call(
        _mixout_kernel,
        out_shape=jax.ShapeDtypeStruct((m, d), F32),
        grid=(m // tm, nj),
        in_specs=[
            pl.BlockSpec((tm, d), lambda i, j: (i, 0)),
            pl.BlockSpec((1, d), lambda i, j: (0, 0)),
            pl.BlockSpec((tm, na), lambda i, j: (i, 0)),
            pl.BlockSpec((tm, nb), lambda i, j: (i, 0)),
            pl.BlockSpec((None, d, tj), lambda i, j: (layer, 0, j)),
            pl.BlockSpec((None, d, tj), lambda i, j: (layer, 0, j + nj)),
            pl.BlockSpec((None, na, tj), lambda i, j: (layer, 0, j)),
            pl.BlockSpec((None, nb, tj), lambda i, j: (layer, 0, j)),
            pl.BlockSpec((None, tj, d), lambda i, j: (layer, j, 0)),
        ],
        out_specs=pl.BlockSpec((tm, d), lambda i, j: (i, 0)),
        scratch_shapes=[pltpu.VMEM((tm, d), BF16)],
        compiler_params=_cparams(("parallel", "arbitrary")),
        name="mixout",
    )(h, gain, oa, ob, w_gates, w_gates, wba, wbb, wo)


def _gdn_kernel(qkvz_ref, ab_ref, convw_ref, alog_ref, dtb_ref, ogain_ref, o_ref, xpad_ref, state_ref,
                *, batch, heads, dk, dv, taps, a_lane, b_lane):
    C = CHUNK
    PAD = V7X_SUBLANES
    nq = heads * dk
    nqkv = 2 * nq + heads * dv

    @pl.when(pl.program_id(0) == 0)
    def _():
        for b in range(batch):
            xpad_ref[b, 0:PAD, :] = jnp.zeros((PAD, nqkv), F32)
        state_ref[...] = jnp.zeros_like(state_ref)

    incl, strict, eye, lvl0, levels = _tri_masks(C)
    tril = jnp.where(incl, 1.0, 0.0).astype(BF16)

    units = []
    qs, ks, vs, ecols, rcols, elasts, bcols, gammas = [], [], [], [], [], [], [], []
    for b in range(batch):
        x = qkvz_ref[b, :, 0:nqkv]
        xpad_ref[b, PAD:PAD + C, :] = x
        acc = None
        for j in range(taps):
            term = xpad_ref[b, pl.ds(PAD - (taps - 1) + j, C), :] * convw_ref[j:j + 1, :]
            acc = term if acc is None else acc + term
        xpad_ref[b, 0:PAD, :] = x[C - PAD:C, :]
        y = acc * _sigmoid(acc)

        ab = ab_ref[b]
        g = -jnp.exp(alog_ref[...]) * _softplus(ab + dtb_ref[...])
        beta = _sigmoid(ab)
        gc = _mm_wide_rhs(tril, g, 3)
        glast = gc[C - 1:C, :]
        e_pos = jnp.exp(gc)
        e_rem = jnp.exp(glast - gc)
        e_last = jnp.exp(glast)
        gc_t = gc.T
        for h in range(heads):
            units.append((b, h))
            qh = y[:, h * dk:(h + 1) * dk]
            kh = y[:, nq + h * dk:nq + (h + 1) * dk]
            qs.append(qh * lax.rsqrt(jnp.sum(qh * qh, axis=-1, keepdims=True) + L2_EPS) * (dk ** -0.5))
            ks.append(kh * lax.rsqrt(jnp.sum(kh * kh, axis=-1, keepdims=True) + L2_EPS))
            vs.append(y[:, 2 * nq + h * dv:2 * nq + (h + 1) * dv])
            la = a_lane + h
            ecols.append(_col(e_pos, la))
            rcols.append(_col(e_rem, la))
            elasts.append(_col(e_last, la))
            bcols.append(_col(beta, b_lane + h))
            diff = _col(gc, la) - gc_t[la:la + 1, :]
            gammas.append(jnp.where(incl, jnp.exp(jnp.where(incl, diff, 0.0)), 0.0))

    us = range(len(units))
    kbs = [ks[i] * bcols[i] for i in us]
    scs = [_mm_nt(jnp.concatenate([kbs[i], qs[i]], axis=0), ks[i]) for i in us]
    ts = _unit_lower_inverse([jnp.where(strict, scs[i][0:C] * gammas[i], 0.0) for i in us], eye, lvl0, levels)
    uws = [_mm(ts[i], jnp.concatenate([vs[i] * bcols[i], kbs[i] * ecols[i]], axis=1)) for i in us]
    ss = [state_ref[i] for i in us]
    wss = [_mm(jnp.concatenate([uws[i][:, dv:dv + dk], qs[i] * ecols[i]], axis=0), ss[i]) for i in us]
    vnews = [uws[i][:, 0:dv] - wss[i][0:C] for i in us]
    os_ = [wss[i][C:2 * C] + _mm(scs[i][C:2 * C] * gammas[i], vnews[i]) for i in us]
    for i in us:
        state_ref[i] = ss[i] * elasts[i] + _mm((ks[i] * rcols[i]).T, vnews[i])
    for i, (b, h) in enumerate(units):
        zh = qkvz_ref[b, :, nqkv + h * dv:nqkv + (h + 1) * dv]
        on = _rmsnorm(os_[i], ogain_ref[...], NORM_EPS)
        o_ref[b, :, h * dv:(h + 1) * dv] = (on * (zh * _sigmoid(zh))).astype(o_ref.dtype)


def _gdn(p, convw, alog_row, dtb_row, ogain, *, heads, dk, dv, ab_block, a_lane, b_lane):
    batch, seq, _ = p.shape
    nqkvz = 2 * heads * dk + 2 * heads * dv
    nqkv = nqkvz - heads * dv
    taps = convw.shape[0]
    const = lambda c: (0, 0)
    return pl.pallas_call(
        functools.partial(_gdn_kernel, batch=batch, heads=heads, dk=dk, dv=dv, taps=taps,
                          a_lane=a_lane, b_lane=b_lane),
        out_shape=jax.ShapeDtypeStruct((batch, seq, heads * dv), BF16),
        grid=(seq // CHUNK,),
        in_specs=[
            pl.BlockSpec((batch, CHUNK, nqkvz), lambda c: (0, c, 0)),
            pl.BlockSpec((batch, CHUNK, V7X_LANES), lambda c: (0, c, ab_block)),
            pl.BlockSpec((taps, nqkv), const),
            pl.BlockSpec((1, V7X_LANES), const),
            pl.BlockSpec((1, V7X_LANES), const),
            pl.BlockSpec((1, dv), const),
        ],
        out_specs=pl.BlockSpec((batch, CHUNK, heads * dv), lambda c: (0, c, 0)),
        scratch_shapes=[
            pltpu.VMEM((batch, CHUNK + V7X_SUBLANES, nqkv), F32),
            pltpu.VMEM((batch * heads, dk, dv), F32),
        ],
        compiler_params=_cparams(("arbitrary",)),
        name="gdn",
    )(p, p, convw, alog_row, dtb_row, ogain)


def _rwkv_kernel(r_ref, k_ref, v_ref, l_ref, mur_ref, muk_ref, muv_ref, mul_ref, w0_ref, wup_ref, a0_ref,
                 aup_ref, gup_ref, kk_ref, ka_ref, rk_ref, lnw_ref, lnb_ref, o_ref,
                 pr_ref, pk_ref, pv_ref, pl_ref, state_ref, *, batch, heads, hd, lw_cols, la_cols):
    C = CHUNK
    PAD = V7X_SUBLANES
    LW = V7X_LANES
    npairs = heads * hd // LW
    rows_all = batch * C
    assert LW == 2 * hd == 2 * C

    @pl.when(pl.program_id(0) == 0)
    def _():
        for ref in (pr_ref, pk_ref, pv_ref, pl_ref):
            ref[...] = jnp.zeros_like(ref)
        state_ref[...] = jnp.zeros_like(state_ref)

    def shift_mix(x_ref, prev_ref, mu_ref):
        parts = []
        for b in range(batch):
            x = x_ref[b]
            row = lax.broadcasted_iota(jnp.int32, x.shape, 0)
            xprev = jnp.where(row == 0, prev_ref[b, PAD - 1:PAD, :], pltpu.roll(x, 1, 0))
            prev_ref[b] = x[C - PAD:C, :]
            parts.append(x + (xprev - x) * mu_ref[...])
        return jnp.concatenate(parts, axis=0)

    r = shift_mix(r_ref, pr_ref, mur_ref)
    k = shift_mix(k_ref, pk_ref, muk_ref)
    v = shift_mix(v_ref, pv_ref, muv_ref)
    lo = shift_mix(l_ref, pl_ref, mul_ref)

    wpre = w0_ref[...] + _mm(jnp.tanh(lo[:, 0:lw_cols]), wup_ref[...])
    lw = -jnp.exp(-_softplus(-wpre) - 0.5)
    a = _sigmoid(a0_ref[...] + _mm(lo[:, lw_cols:lw_cols + la_cols], aup_ref[...]))
    gate = _mm(_sigmoid(lo[:, lw_cols + la_cols:]), gup_ref[...])

    incl, strict, eye, lvl0, levels = _tri_masks(2 * C)
    ti = lax.broadcasted_iota(jnp.int32, (rows_all, rows_all), 0)
    tj = lax.broadcasted_iota(jnp.int32, (rows_all, rows_all), 1)
    tril = jnp.where(((ti // C) == (tj // C)) & (tj <= ti), 1.0, 0.0).astype(BF16)
    li = lax.broadcasted_iota(jnp.int32, (LW, LW), 0)
    lj = lax.broadcasted_iota(jnp.int32, (LW, LW), 1)
    headsum = jnp.where((li // hd) == (lj // hd), 1.0, 0.0).astype(BF16)
    m0 = lax.broadcasted_iota(jnp.int32, (C, LW), 1) < hd

    kkraw = k * kk_ref[...]
    k2 = k * (1.0 + (a - 1.0) * ka_ref[...])
    cum = _mm_wide_rhs(tril, lw, 3)
    clast = jnp.concatenate(
        [jnp.broadcast_to(cum[b * C + C - 1:b * C + C, :], (C, cum.shape[1])) for b in range(batch)], axis=0)
    e_pos = jnp.exp(cum)
    e_neg = jnp.exp(-cum)
    e_exc = jnp.exp(cum - lw)
    e_rem = jnp.exp(clast - cum)
    wc = jnp.exp(clast)
    rkr = r * k2 * rk_ref[...]

    def stack(x):
        return jnp.concatenate([jnp.where(m0, x, 0.0), jnp.where(m0, 0.0, x)], axis=0)

    units = [(b, p) for b in range(batch) for p in range(npairs)]
    us = range(len(units))
    win = [(slice(b * C, (b + 1) * C), slice(p * LW, (p + 1) * LW)) for b, p in units]
    sums = [_mm_wide_lhs(jnp.concatenate([kkraw[w] * kkraw[w], rkr[w]], axis=0), headsum, 1) for w in win]
    kks = [kkraw[w] * lax.rsqrt(sums[i][0:C] + L2_EPS) for i, w in enumerate(win)]
    kkas = [kks[i] * a[w] for i, w in enumerate(win)]
    a_ss = [stack(-kks[i] * e_exc[w]) for i, w in enumerate(win)]
    r_ss = [stack(r[w] * e_pos[w]) for w in win]
    b_ss = [stack(kkas[i] * e_neg[w]) for i, w in enumerate(win)]
    k_ss = [stack(k2[w] * e_neg[w]) for w in win]
    v_ss = [stack(v[w]) for w in win]
    scs = [_mm_nt(jnp.concatenate([a_ss[i], r_ss[i]], axis=0), jnp.concatenate([b_ss[i], k_ss[i]], axis=0))
           for i in us]
    lvs = [_mm(jnp.where(strict, scs[i][0:2 * C, 2 * C:4 * C], 0.0), v_ss[i]) for i in us]
    ts = _unit_lower_inverse([jnp.where(strict, -scs[i][0:2 * C, 0:2 * C], 0.0) for i in us], eye, lvl0, levels)
    tts = [_mm(ts[i], jnp.concatenate([a_ss[i], lvs[i]], axis=1)) for i in us]
    qs = [state_ref[i] for i in us]
    pqs = [_mm_nt(jnp.concatenate([tts[i][:, 0:LW], r_ss[i]], axis=0), qs[i]) for i in us]
    uvs = [jnp.concatenate([pqs[i][0:2 * C] + tts[i][:, LW:2 * LW], v_ss[i]], axis=0) for i in us]
    for i, w in enumerate(win):
        bk = jnp.concatenate([stack(kkas[i] * e_rem[w]), stack(k2[w] * e_rem[w])], axis=0)
        state_ref[i] = qs[i] * wc[w][0:1, :] + _mm(uvs[i].T, bk)
    y_ss = [pqs[i][2 * C:4 * C]
            + _mm(jnp.concatenate([jnp.where(incl, scs[i][2 * C:4 * C, 0:2 * C], 0.0),
                                   jnp.where(incl, scs[i][2 * C:4 * C, 2 * C:4 * C], 0.0)], axis=1), uvs[i])
            for i in us]
    yps = [y_s[0:C] + y_s[C:2 * C] for y_s in y_ss]

    dlts = [yp - _mm_wide_lhs(yp, headsum, 2) * (1.0 / hd) for yp in yps]
    vars_ = [_mm_wide_lhs(dlt * dlt, headsum, 1) * (1.0 / hd) for dlt in dlts]
    for i, ((b, p), w) in enumerate(zip(units, win)):
        sl = w[1]
        yn = dlts[i] * lax.rsqrt(vars_[i] + RW_GN_EPS) * lnw_ref[:, sl] + lnb_ref[:, sl]
        bonus = sums[i][C:2 * C] * v[w]
        o_ref[b, :, sl] = ((yn + bonus) * gate[w]).astype(o_ref.dtype)


def _rwkv(p, mu_r, mu_k, mu_v, mu_l, w0, wup, a0, aup, gup, k_k, k_a, r_k, ln_w, ln_b,
          *, heads, hd, r_block, lora_block, lora_width):
    batch, seq, _ = p.shape
    width = heads * hd
    const = lambda c: (0, 0)
    vec = pl.BlockSpec((1, width), const)
    lw_cols, la_cols, lg_cols = wup.shape[0], aup.shape[0], gup.shape[0]
    assert lw_cols + la_cols + lg_cols == lora_width
    return pl.pallas_call(
        functools.partial(_rwkv_kernel, batch=batch, heads=heads, hd=hd, lw_cols=lw_cols, la_cols=la_cols),
        out_shape=jax.ShapeDtypeStruct((batch, seq, width), BF16),
        grid=(seq // CHUNK,),
        in_specs=[
            pl.BlockSpec((batch, CHUNK, width), lambda c: (0, c, r_block)),
            pl.BlockSpec((batch, CHUNK, width), lambda c: (0, c, r_block + 1)),
            pl.BlockSpec((batch, CHUNK, width), lambda c: (0, c, r_block + 2)),
            pl.BlockSpec((batch, CHUNK, lora_width), lambda c: (0, c, lora_block)),
            vec, vec, vec,
            pl.BlockSpec((1, lora_width), const),
            vec,
            pl.BlockSpec((lw_cols, width), const),
            vec,
            pl.BlockSpec((la_cols, width), const),
            pl.BlockSpec((lg_cols, width), const),
            vec, vec, vec, vec, vec,
        ],
        out_specs=pl.BlockSpec((batch, CHUNK, width), lambda c: (0, c, 0)),
        scratch_shapes=[
            pltpu.VMEM((batch, V7X_SUBLANES, width), F32),
            pltpu.VMEM((batch, V7X_SUBLANES, width), F32),
            pltpu.VMEM((batch, V7X_SUBLANES, width), F32),
            pltpu.VMEM((batch, V7X_SUBLANES, lora_width), F32),
            pltpu.VMEM((batch * width // V7X_LANES, V7X_LANES, V7X_LANES), F32),
        ],
        compiler_params=_cparams(("arbitrary",)),
        name="rwkv",
    )(p, p, p, p, mu_r, mu_k, mu_v, mu_l, w0, wup, a0, aup, gup, k_k, k_a, r_k, ln_w, ln_b)


def _round_up(n, mult):
    return -(-n // mult) * mult


def _pad_cols(w, total):
    return jnp.concatenate([w, jnp.zeros((w.shape[0], total - w.shape[1]), w.dtype)], axis=1)


def _pad_rows(w, total):
    return jnp.concatenate([w, jnp.zeros((total - w.shape[0], w.shape[1]), w.dtype)], axis=0)


def _lane_row(vals, start, total):
    return jnp.zeros((1, total), F32).at[0, start:start + vals.shape[0]].set(vals)


def kernel(x, ffn1_norm, ffn1_w_gate, ffn1_w_up, ffn1_w_down, mix_norm, w_in, gdn_conv, gdn_a_log, gdn_dt_bias, gdn_out_norm, rw_mu, rw_w0, rw_w_up, rw_a0, rw_a_up, rw_g_up, rw_k_k, rw_k_a, rw_r_k, rw_ln_w, rw_ln_b, w_branch_a, w_branch_b, w_out, ffn2_norm, ffn2_w_gate, ffn2_w_up, ffn2_w_down, final_norm):
    batch, seq, d = x.shape
    depth = w_in.shape[0]
    gdn_w = w_branch_a.shape[1]
    rw_w = w_branch_b.shape[1]
    dk = gdn_w // GDN_HEADS
    dv = gdn_out_norm.shape[1]
    hd = rw_w // RW_HEADS
    lw_, la_, lg_ = rw_w_up.shape[1], rw_a_up.shape[1], rw_g_up.shape[1]
    n_qkvz = 4 * gdn_w
    n_ab = 2 * GDN_HEADS
    n_slab = 3 * rw_w + lw_ + la_ + lg_
    off_ab = n_qkvz
    off_slab = n_qkvz + n_ab
    off_lora = off_slab + 3 * rw_w
    off_ga = off_slab + n_slab

    lw_cols = _round_up(lw_ + n_ab, V7X_LANES)
    la_cols = _round_up(la_, V7X_LANES)
    lg_cols = _round_up(lg_, V7X_LANES)
    lora_width = lw_cols + la_cols + lg_cols
    n_mix = n_qkvz + 3 * rw_w + lora_width
    assert (n_qkvz + 3 * rw_w) % lora_width == 0 and n_qkvz % rw_w == 0
    ab_block = (n_qkvz + 3 * rw_w) // V7X_LANES
    a_lane = lw_
    b_lane = lw_ + GDN_HEADS

    m = batch * seq
    h = x.reshape(m, d)
    row = lambda t: t.reshape(1, -1).astype(F32)

    zeros = lambda n: jnp.zeros((depth, d, n), w_in.dtype)
    w_mix_all = jnp.concatenate(
        [w_in[:, :, 0:off_ab], w_in[:, :, off_slab:off_lora],
         w_in[:, :, off_lora:off_lora + lw_], w_in[:, :, off_ab:off_ab + n_ab], zeros(lw_cols - lw_ - n_ab),
         w_in[:, :, off_lora + lw_:off_lora + lw_ + la_], zeros(la_cols - la_),
         w_in[:, :, off_lora + lw_ + la_:off_ga], zeros(lg_cols - lg_)], axis=2).astype(BF16)
    w_gates_all = w_in[:, :, off_ga:off_ga + 2 * d].astype(BF16)
    f1g, f1u, f1d = ffn1_w_gate.astype(BF16), ffn1_w_up.astype(BF16), ffn1_w_down.astype(BF16)
    f2g, f2u, f2d = ffn2_w_gate.astype(BF16), ffn2_w_up.astype(BF16), ffn2_w_down.astype(BF16)
    wba_all, wbb_all, wo_all = w_branch_a.astype(BF16), w_branch_b.astype(BF16), w_out.astype(BF16)
    for l in range(depth):
        h = _ffn(h, row(ffn1_norm[l]), f1g, f1u, f1d, row(final_norm), layer=l, final=False, tm=1024, tf=512)

        p = _inproj(h, row(mix_norm[l]), w_mix_all, layer=l, tm=1024, tn=n_mix // 5).reshape(batch, seq, n_mix)

        oa = _gdn(p, gdn_conv[l], _lane_row(gdn_a_log[l], a_lane, V7X_LANES),
                  _lane_row(gdn_dt_bias[l], a_lane, V7X_LANES), row(gdn_out_norm[l]),
                  heads=GDN_HEADS, dk=dk, dv=dv, ab_block=ab_block, a_lane=a_lane, b_lane=b_lane)

        mu = rw_mu[l]
        mu_l = jnp.concatenate(
            [_lane_row(mu[3 * rw_w:3 * rw_w + lw_], 0, lw_cols),
             _lane_row(mu[3 * rw_w + lw_:3 * rw_w + lw_ + la_], 0, la_cols),
             _lane_row(mu[3 * rw_w + lw_ + la_:], 0, lg_cols)], axis=1)
        ob = _rwkv(p, row(mu[0:rw_w]), row(mu[rw_w:2 * rw_w]), row(mu[2 * rw_w:3 * rw_w]), mu_l,
                   row(rw_w0[l]), _pad_rows(rw_w_up[l], lw_cols).astype(BF16),
                   row(rw_a0[l]), _pad_rows(rw_a_up[l], la_cols).astype(BF16),
                   _pad_rows(rw_g_up[l], lg_cols).astype(BF16),
                   row(rw_k_k[l]), row(rw_k_a[l]), row(rw_r_k[l]), row(rw_ln_w[l]), row(rw_ln_b[l]),
                   heads=RW_HEADS, hd=hd, r_block=n_qkvz // rw_w,
                   lora_block=(n_qkvz + 3 * rw_w) // lora_width, lora_width=lora_width)

        h = _mixout(h, row(mix_norm[l]), oa.reshape(m, gdn_w), ob.reshape(m, rw_w), w_gates_all,
                    wba_all, wbb_all, wo_all, layer=l, tm=512, tj=512)

        h = _ffn(h, row(ffn2_norm[l]), f2g, f2u, f2d, row(final_norm), layer=l, final=(l == depth - 1),
                 tm=1024, tf=512)

    return h.reshape(batch, seq, d)
```
